```python
import jax, jax.numpy as jnp
from jax import lax
import numpy as np

D_MODEL = 1024
BATCH = 8
SEQ = 4096
DEPTH = 2

N_HEADS = 8
QK_NOPE_DIM = 64
QK_ROPE_DIM = 32
QK_HEAD_DIM = QK_NOPE_DIM + QK_ROPE_DIM
V_HEAD_DIM = 64
Q_LORA_RANK = 384
KV_LORA_RANK = 256
ROPE_THETA = 10000.0
Q_BLOCK = 128
ATTN_SCALE = QK_HEAD_DIM ** -0.5

POOL_WINDOWS = (2, 4, 8, 16)
N_POOL_GROUPS = 4
POOL_DIM = D_MODEL // 2
POOL_GROUP_DIM = POOL_DIM // N_POOL_GROUPS

N_BRANCHES = 2
ATTN_OUT_DIM = N_HEADS * V_HEAD_DIM

IN_Q_END = Q_LORA_RANK
IN_KV_END = IN_Q_END + KV_LORA_RANK
IN_KROPE_END = IN_KV_END + QK_ROPE_DIM
IN_POOL_END = IN_KROPE_END + POOL_DIM
D_IN_PROJ = IN_POOL_END + N_BRANCHES * D_MODEL

N_EXPERTS = 16
N_EXPERT_GROUPS = 4
EXPERTS_PER_GROUP = N_EXPERTS // N_EXPERT_GROUPS
GROUP_SCORE_TOPK = 2
TOP_K = 2
D_EXPERT = 256

EPS = 1e-6

kernel_name = "hybrid_mla_pool_grouped_moe_encoder"


def rms_norm(x, g):
    xf = x.astype(jnp.float32)
    y = xf * lax.rsqrt(jnp.mean(xf * xf, axis=-1, keepdims=True) + EPS)
    return (y * g.astype(jnp.float32)).astype(x.dtype)


def rope_cos_sin(positions):
    inv_freq = 1.0 / (ROPE_THETA ** (jnp.arange(0, QK_ROPE_DIM, 2, dtype=jnp.float32) / QK_ROPE_DIM))
    ang = positions.astype(jnp.float32)[..., None] * inv_freq
    return jnp.cos(ang), jnp.sin(ang)


def apply_rope(x, cos, sin):
    half = x.shape[-1] // 2
    x1 = x[..., :half].astype(jnp.float32)
    x2 = x[..., half:].astype(jnp.float32)
    out = jnp.concatenate([x1 * cos - x2 * sin, x1 * sin + x2 * cos], axis=-1)
    return out.astype(x.dtype)


def mla(c_q, c_kv, k_rope, positions, q_a_g, kv_a_g, w_uq, w_ukv, q_norm_g, k_norm_g):
    B, S, _ = c_q.shape
    q = (rms_norm(c_q, q_a_g) @ w_uq).reshape(B, S, N_HEADS, QK_HEAD_DIM)
    kv = (rms_norm(c_kv, kv_a_g) @ w_ukv).reshape(B, S, N_HEADS, QK_NOPE_DIM + V_HEAD_DIM)
    k_nope, v = kv[..., :QK_NOPE_DIM], kv[..., QK_NOPE_DIM:]
    k_pe = jnp.broadcast_to(k_rope[:, :, None, :], (B, S, N_HEADS, QK_ROPE_DIM))
    k = jnp.concatenate([k_nope, k_pe], axis=-1)
    q = rms_norm(q, q_norm_g)
    k = rms_norm(k, k_norm_g)
    cos, sin = rope_cos_sin(positions)
    cos, sin = cos[:, :, None, :], sin[:, :, None, :]
    q = jnp.concatenate([q[..., :QK_NOPE_DIM], apply_rope(q[..., QK_NOPE_DIM:], cos, sin)], axis=-1)
    k = jnp.concatenate([k[..., :QK_NOPE_DIM], apply_rope(k[..., QK_NOPE_DIM:], cos, sin)], axis=-1)
    q = q.transpose(0, 2, 1, 3)
    k = k.transpose(0, 2, 1, 3)
    v = v.transpose(0, 2, 1, 3)
    n_blk = S // Q_BLOCK
    q_blocks = q.reshape(B, N_HEADS, n_blk, Q_BLOCK, QK_HEAD_DIM).transpose(2, 0, 1, 3, 4)

    def attend(qb):
        s = jnp.einsum('bhqd,bhkd->bhqk', qb, k, preferred_element_type=jnp.float32) * ATTN_SCALE
        p = jax.nn.softmax(s, axis=-1)
        return jnp.einsum('bhqk,bhkd->bhqd', p.astype(v.dtype), v)

    o = lax.map(attend, q_blocks)
    return o.transpose(1, 0, 3, 2, 4).reshape(B, S, ATTN_OUT_DIM)


def multiscale_pool(u, w_group, scale):
    B, S, C = u.shape
    uf = u.astype(jnp.float32)
    cs = jnp.concatenate([jnp.zeros((B, 1, C), jnp.float32), jnp.cumsum(uf, axis=1)], axis=1)
    t = jnp.arange(S)
    outs = []
    for gi, w in enumerate(POOL_WINDOWS):
        left = w // 2
        right = w - 1 - left
        hi = jnp.clip(t + right + 1, 0, S)
        lo = jnp.clip(t - left, 0, S)
        csg = cs[..., gi * POOL_GROUP_DIM:(gi + 1) * POOL_GROUP_DIM]
        sums = jnp.take(csg, hi, axis=1) - jnp.take(csg, lo, axis=1)
        cnt = (hi - lo).astype(jnp.float32)[None, :, None]
        outs.append(sums / cnt)
    pooled = jnp.concatenate(outs, axis=-1) - uf
    pooled = pooled.reshape(B, S, N_POOL_GROUPS, POOL_GROUP_DIM)
    mixed = jnp.einsum('bsgc,gcd->bsgd', pooled, w_group.astype(jnp.float32)).reshape(B, S, C)
    return (mixed * scale.astype(jnp.float32)).astype(u.dtype)


def route(h, w_router, router_bias):
    N = h.shape[0]
    scores = jax.nn.sigmoid(h.astype(jnp.float32) @ w_router.astype(jnp.float32))
    biased = scores + router_bias.astype(jnp.float32)
    grp = biased.reshape(N, N_EXPERT_GROUPS, EXPERTS_PER_GROUP)
    group_score = jnp.sum(lax.top_k(grp, GROUP_SCORE_TOPK)[0], axis=-1)
    best_group = jnp.argmax(group_score, axis=-1)
    in_group = jax.nn.one_hot(best_group, N_EXPERT_GROUPS, dtype=jnp.bool_)
    in_group = jnp.repeat(in_group, EXPERTS_PER_GROUP, axis=-1)
    masked = jnp.where(in_group, biased, -jnp.inf)
    _, idx = lax.top_k(masked, TOP_K)
    sel = jnp.take_along_axis(scores, idx, axis=-1)
    w = sel / jnp.sum(sel, axis=-1, keepdims=True)
    return jnp.sum(jax.nn.one_hot(idx, N_EXPERTS, dtype=jnp.float32) * w[..., None], axis=1)


def moe(h, gates, w_gate, w_up, w_down):
    y = jnp.zeros(h.shape, jnp.float32)
    for e in range(N_EXPERTS):
        act = jax.nn.silu(h @ w_gate[e]) * (h @ w_up[e])
        y = y + gates[:, e:e + 1] * (act @ w_down[e]).astype(jnp.float32)
    return y.astype(h.dtype)


def setup_inputs(seed: int = 0) -> dict:
    key = jax.random.key(seed)
    ks = jax.random.split(key, 24)
    f32 = jnp.float32

    def nrm(k, shape, fan_in, gain=1.0):
        return jax.random.normal(k, shape, f32) * (gain * fan_in ** -0.5)

    def gain(k, shape):
        return 1.0 + 0.02 * jax.random.normal(k, shape, f32)

    x = jax.random.normal(ks[0], (BATCH, SEQ, D_MODEL), f32)
    offsets = jax.random.randint(ks[1], (BATCH, 1), 0, 1024, dtype=jnp.int32)
    positions = offsets + jnp.arange(SEQ, dtype=jnp.int32)[None, :]
    return {
        "x": x,
        "positions": positions,
        "norm1_g": gain(ks[2], (DEPTH, D_MODEL)),
        "w_in": nrm(ks[3], (DEPTH, D_MODEL, D_IN_PROJ), D_MODEL),
        "q_a_g": gain(ks[4], (DEPTH, Q_LORA_RANK)),
        "kv_a_g": gain(ks[5], (DEPTH, KV_LORA_RANK)),
        "w_uq": nrm(ks[6], (DEPTH, Q_LORA_RANK, N_HEADS * QK_HEAD_DIM), Q_LORA_RANK),
        "w_ukv": nrm(ks[7], (DEPTH, KV_LORA_RANK, N_HEADS * (QK_NOPE_DIM + V_HEAD_DIM)), KV_LORA_RANK),
        "q_norm_g": gain(ks[8], (DEPTH, QK_HEAD_DIM)),
        "k_norm_g": gain(ks[9], (DEPTH, QK_HEAD_DIM)),
        "w_pool": nrm(ks[10], (DEPTH, N_POOL_GROUPS, POOL_GROUP_DIM, POOL_GROUP_DIM), POOL_GROUP_DIM),
        "pool_scale": gain(ks[11], (DEPTH, POOL_DIM)),
        "w_branch_attn": nrm(ks[12], (DEPTH, ATTN_OUT_DIM, D_MODEL), ATTN_OUT_DIM),
        "w_branch_pool": nrm(ks[13], (DEPTH, POOL_DIM, D_MODEL), POOL_DIM),
        "w_out": nrm(ks[14], (DEPTH, D_MODEL, D_MODEL), D_MODEL, 0.5),
        "norm2_g": gain(ks[15], (DEPTH, D_MODEL)),
        "w_router": nrm(ks[16], (D_MODEL, N_EXPERTS), D_MODEL),
        "router_bias": 0.01 * jax.random.normal(ks[17], (N_EXPERTS,), f32),
        "w_expert_gate": nrm(ks[18], (DEPTH, N_EXPERTS, D_MODEL, D_EXPERT), D_MODEL),
        "w_expert_up": nrm(ks[19], (DEPTH, N_EXPERTS, D_MODEL, D_EXPERT), D_MODEL),
        "w_expert_down": nrm(ks[20], (DEPTH, N_EXPERTS, D_EXPERT, D_MODEL), D_EXPERT, 0.5),
    }


def reference(x, positions, norm1_g, w_in, q_a_g, kv_a_g, w_uq, w_ukv, q_norm_g, k_norm_g,
              w_pool, pool_scale, w_branch_attn, w_branch_pool, w_out, norm2_g,
              w_router, router_bias, w_expert_gate, w_expert_up, w_expert_down):
    B, S, D = x.shape
    for l in range(DEPTH):
        h = rms_norm(x, norm1_g[l])
        z = h @ w_in[l]
        c_q = z[..., :IN_Q_END]
        c_kv = z[..., IN_Q_END:IN_KV_END]
        k_rope = z[..., IN_KV_END:IN_KROPE_END]
        u = z[..., IN_KROPE_END:IN_POOL_END]
        gate_logits = z[..., IN_POOL_END:]
        a = mla(c_q, c_kv, k_rope, positions, q_a_g[l], kv_a_g[l], w_uq[l], w_ukv[l],
                q_norm_g[l], k_norm_g[l]) @ w_branch_attn[l]
        p = multiscale_pool(u, w_pool[l], pool_scale[l]) @ w_branch_pool[l]
        g = jax.nn.sigmoid(gate_logits.astype(jnp.float32)).reshape(B, S, N_BRANCHES, D)
        m = g[..., 0, :] * a.astype(jnp.float32) + g[..., 1, :] * p.astype(jnp.float32)
        x = x + m.astype(x.dtype) @ w_out[l]
        h2 = rms_norm(x, norm2_g[l]).reshape(B * S, D)
        gates = route(h2, w_router, router_bias)
        x = x + moe(h2, gates, w_expert_gate[l], w_expert_up[l], w_expert_down[l]).reshape(B, S, D)
    return x
```

```python
import functools

import jax
import jax.numpy as jnp
from jax import lax
from jax.experimental import pallas as pl
from jax.experimental.pallas import tpu as pltpu

D_MODEL = 1024
N_HEADS = 8
QK_NOPE_DIM = 64
QK_ROPE_DIM = 32
QK_HEAD_DIM = QK_NOPE_DIM + QK_ROPE_DIM
V_HEAD_DIM = 64
Q_LORA_RANK = 384
KV_LORA_RANK = 256
ROPE_THETA = 10000.0
ATTN_SCALE = QK_HEAD_DIM ** -0.5
POOL_WINDOWS = (2, 4, 8, 16)
N_POOL_GROUPS = 4
POOL_DIM = D_MODEL // 2
POOL_GROUP_DIM = POOL_DIM // N_POOL_GROUPS
ATTN_OUT_DIM = N_HEADS * V_HEAD_DIM
N_EXPERTS = 16
N_EXPERT_GROUPS = 4
EXPERTS_PER_GROUP = N_EXPERTS // N_EXPERT_GROUPS
D_EXPERT = 256
EPS = 1e-6

LANES = 128
SUBLANES = 8
HEAD_PAD = LANES
QK_PAD_DIM = N_HEADS * HEAD_PAD
HEADS_PER_STEP = 2
POOL_HALO = 8

_C_Q = (0, Q_LORA_RANK)
_C_KV = (_C_Q[1], _C_Q[1] + KV_LORA_RANK)
_C_KR = (_C_KV[1], _C_KV[1] + HEAD_PAD)
_C_U = (_C_KR[1], _C_KR[1] + POOL_DIM)
_C_G = (_C_U[1], _C_U[1] + 2 * D_MODEL)
D_IN_PAD = _C_G[1]

VMEM_LIMIT = 56 * 1024 * 1024

_F32 = jnp.float32
_BF16 = jnp.bfloat16


def _const_spec(shape):
    nd = len(shape)
    return pl.BlockSpec(shape, lambda *_: (0,) * nd, pipeline_mode=pl.Buffered(1))


def _sigmoid(x):
    return 1.0 / (1.0 + jnp.exp(-x))


def _rope_table_kernel(pos_ref, freq_ref, c_ref, s1_ref, s2_ref):
    ang = pos_ref[...] * freq_ref[...]
    cos = jnp.cos(ang)
    sin = jnp.sin(ang)
    lane = lax.broadcasted_iota(jnp.int32, ang.shape, 1)
    half = QK_ROPE_DIM // 2
    first = (lane >= QK_NOPE_DIM) & (lane < QK_NOPE_DIM + half)
    second = (lane >= QK_NOPE_DIM + half) & (lane < QK_HEAD_DIM)
    c_ref[...] = jnp.where(first | second, cos, 1.0)
    s1_ref[...] = jnp.where(first, -sin, 0.0)
    s2_ref[...] = jnp.where(second, sin, 0.0)


def _rope_table_call(pos, freq, tm=1024):
    n = pos.shape[0]
    tab = jax.ShapeDtypeStruct((n, HEAD_PAD), _F32)
    return pl.pallas_call(
        _rope_table_kernel,
        grid=(n // tm,),
        in_specs=[pl.BlockSpec((tm, 1), lambda i: (i, 0)), _const_spec((1, HEAD_PAD))],
        out_specs=[pl.BlockSpec((tm, HEAD_PAD), lambda i: (i, 0))] * 3,
        out_shape=[tab, tab, tab],
        compiler_params=pltpu.CompilerParams(dimension_semantics=("parallel",)),
        name="rope_table",
    )(pos, freq)


def _rope(x, c, s1, s2):
    half = QK_ROPE_DIM // 2
    return (x * c + pltpu.roll(x, HEAD_PAD - half, 1) * s1 + pltpu.roll(x, half, 1) * s2)


def _in_proj_kernel(x_ref, c_ref, s1_ref, s2_ref, g1_ref, win_ref, qag_ref, kvag_ref,
                    wuq_ref, wuk_ref, wuv_ref, gq_ref, gk_ref,
                    q_ref, k_ref, v_ref, u_ref, gl_ref):
    x = x_ref[...]
    h = x * lax.rsqrt(jnp.mean(x * x, axis=-1, keepdims=True) + EPS) * g1_ref[...]
    h = h.astype(_BF16)

    def proj(cols):
        return jnp.dot(h, win_ref[:, cols[0]:cols[1]], preferred_element_type=_F32)

    gl_ref[...] = proj(_C_G).astype(_BF16)
    u_ref[...] = proj(_C_U)

    c, s1, s2 = c_ref[...], s1_ref[...], s2_ref[...]

    cq = proj(_C_Q)
    cqn = cq * lax.rsqrt(jnp.mean(cq * cq, axis=-1, keepdims=True) + EPS) * qag_ref[...]
    qp = jnp.dot(cqn.astype(_BF16), wuq_ref[...], preferred_element_type=_F32)
    gq = gq_ref[...] * ATTN_SCALE
    for hd in range(N_HEADS):
        sl = slice(hd * HEAD_PAD, (hd + 1) * HEAD_PAD)
        qh = qp[:, sl]
        r = lax.rsqrt(jnp.sum(qh * qh, axis=-1, keepdims=True) / QK_HEAD_DIM + EPS)
        q_ref[:, sl] = _rope(qh * r * gq, c, s1, s2).astype(_BF16)

    ckv = proj(_C_KV)
    ckvn = ckv * lax.rsqrt(jnp.mean(ckv * ckv, axis=-1, keepdims=True) + EPS) * kvag_ref[...]
    ckvn = ckvn.astype(_BF16)
    v_ref[...] = jnp.dot(ckvn, wuv_ref[...], preferred_element_type=_F32).astype(_BF16)
    kp = jnp.dot(ckvn, wuk_ref[...], preferred_element_type=_F32)
    kr = proj(_C_KR)
    gk = gk_ref[...]
    for hd in range(N_HEADS):
        sl = slice(hd * HEAD_PAD, (hd + 1) * HEAD_PAD)
        kh = kp[:, sl] + kr
        r = lax.rsqrt(jnp.sum(kh * kh, axis=-1, keepdims=True) / QK_HEAD_DIM + EPS)
        k_ref[:, sl] = _rope(kh * r * gk, c, s1, s2).astype(_BF16)


def _in_proj_call(x, tabs, g1, win, qag, kvag, wuq, wuk, wuv, gq, gk, tm=512):
    n = x.shape[0]
    row = lambda w: pl.BlockSpec((tm, w), lambda i: (i, 0))
    return pl.pallas_call(
        _in_proj_kernel,
        grid=(n // tm,),
        in_specs=[row(D_MODEL), row(HEAD_PAD), row(HEAD_PAD), row(HEAD_PAD),
                  _const_spec(g1.shape), _const_spec(win.shape), _const_spec(qag.shape),
                  _const_spec(kvag.shape), _const_spec(wuq.shape), _const_spec(wuk.shape),
                  _const_spec(wuv.shape), _const_spec(gq.shape), _const_spec(gk.shape)],
        out_specs=[row(QK_PAD_DIM), row(QK_PAD_DIM), row(ATTN_OUT_DIM), row(POOL_DIM),
                   row(2 * D_MODEL)],
        out_shape=[jax.ShapeDtypeStruct((n, QK_PAD_DIM), _BF16),
                   jax.ShapeDtypeStruct((n, QK_PAD_DIM), _BF16),
                   jax.ShapeDtypeStruct((n, ATTN_OUT_DIM), _BF16),
                   jax.ShapeDtypeStruct((n, POOL_DIM), _F32),
                   jax.ShapeDtypeStruct((n, 2 * D_MODEL), _BF16)],
        compiler_params=pltpu.CompilerParams(dimension_semantics=("parallel",),
                                             vmem_limit_bytes=VMEM_LIMIT),
        name="in_proj",
    )(x, *tabs, g1, win, qag, kvag, wuq, wuk, wuv, gq, gk)


def _attention_kernel(q_ref, k_ref, v_ref, o_ref, *, tk):
    tq = q_ref.shape[1]
    n_kv = k_ref.shape[1] // tk
    outs = []
    for hd in range(HEADS_PER_STEP):
        sl = slice(hd * HEAD_PAD, (hd + 1) * HEAD_PAD)
        q = q_ref[0, :, sl]

        def body(j, carry, sl=sl, q=q):
            m, l, acc = carry
            rows = pl.ds(pl.multiple_of(j * tk, tk), tk)
            s = lax.dot_general(q, k_ref[0, rows, sl], (((1,), (1,)), ((), ())),
                                preferred_element_type=_F32)
            m_new = jnp.maximum(m, jnp.max(s, axis=-1, keepdims=True))
            alpha = jnp.exp(m - m_new)
            p = jnp.exp(s - m_new)
            l = alpha * l + jnp.sum(p, axis=-1, keepdims=True)
            acc = alpha * acc + jnp.dot(p.astype(_BF16), v_ref[0, rows, :],
                                        preferred_element_type=_F32)
            return m_new, l, acc

        init = (jnp.full((tq, 1), -jnp.inf, _F32), jnp.zeros((tq, 1), _F32),
                jnp.zeros((tq, HEADS_PER_STEP * V_HEAD_DIM), _F32))
        _, l, acc = lax.fori_loop(0, n_kv, body, init)
        outs.append(acc / l)
    lane = lax.broadcasted_iota(jnp.int32, outs[0].shape, 1)
    o_ref[0] = jnp.where(lane < V_HEAD_DIM, outs[0], outs[1]).astype(o_ref.dtype)


def _attention_call(q, k, v, tq=512, tk=512):
    b, s, _ = q.shape
    qw = HEADS_PER_STEP * HEAD_PAD
    vw = HEADS_PER_STEP * V_HEAD_DIM
    return pl.pallas_call(
        functools.partial(_attention_kernel, tk=tk),
        grid=(b, N_HEADS // HEADS_PER_STEP, s // tq),
        in_specs=[pl.BlockSpec((1, tq, qw), lambda bi, hp, qi: (bi, qi, hp)),
                  pl.BlockSpec((1, s, qw), lambda bi, hp, qi: (bi, 0, hp)),
                  pl.BlockSpec((1, s, vw), lambda bi, hp, qi: (bi, 0, hp))],
        out_specs=pl.BlockSpec((1, tq, vw), lambda bi, hp, qi: (bi, qi, hp)),
        out_shape=jax.ShapeDtypeStruct((b, s, ATTN_OUT_DIM), _BF16),
        compiler_params=pltpu.CompilerParams(
            dimension_semantics=("parallel", "parallel", "arbitrary"),
            vmem_limit_bytes=VMEM_LIMIT),
        name="attention",
    )(q, k, v)


def _route_rows(logit_rows, bias_ref):
    scores = [_sigmoid(r) for r in logit_rows]
    biased = [scores[e] + bias_ref[e] for e in range(N_EXPERTS)]
    gscore = []
    for g in range(N_EXPERT_GROUPS):
        a, b, c, d = biased[g * EXPERTS_PER_GROUP:(g + 1) * EXPERTS_PER_GROUP]
        hi1, lo1 = jnp.maximum(a, b), jnp.minimum(a, b)
        hi2, lo2 = jnp.maximum(c, d), jnp.minimum(c, d)
        top = jnp.maximum(hi1, hi2)
        second = jnp.maximum(jnp.minimum(hi1, hi2), jnp.maximum(lo1, lo2))
        gscore.append(top + second)
    best = gscore[0]
    bg = jnp.zeros_like(best, dtype=jnp.int32)
    for g in range(1, N_EXPERT_GROUPS):
        better = gscore[g] > best
        best = jnp.where(better, gscore[g], best)
        bg = jnp.where(better, g, bg)
    vb, vs = [], []
    for i in range(EXPERTS_PER_GROUP):
        b_i, s_i = biased[i], scores[i]
        for g in range(1, N_EXPERT_GROUPS):
            pick = bg == g
            b_i = jnp.where(pick, biased[g * EXPERTS_PER_GROUP + i], b_i)
            s_i = jnp.where(pick, scores[g * EXPERTS_PER_GROUP + i], s_i)
        vb.append(b_i)
        vs.append(s_i)
    i1 = jnp.zeros_like(bg)
    b1 = vb[0]
    for i in range(1, EXPERTS_PER_GROUP):
        better = vb[i] > b1
        b1 = jnp.where(better, vb[i], b1)
        i1 = jnp.where(better, i, i1)
    i2 = jnp.full_like(bg, -1)
    b2 = jnp.full_like(b1, -jnp.inf)
    for i in range(EXPERTS_PER_GROUP):
        better = (i1 != i) & ((vb[i] > b2) | (i2 < 0))
        b2 = jnp.where(better, vb[i], b2)
        i2 = jnp.where(better, i, i2)
    s1 = vs[0]
    s2 = vs[0]
    for i in range(1, EXPERTS_PER_GROUP):
        s1 = jnp.where(i1 == i, vs[i], s1)
        s2 = jnp.where(i2 == i, vs[i], s2)
    denom = s1 + s2
    gates = []
    for e in range(N_EXPERTS):
        g, i = divmod(e, EXPERTS_PER_GROUP)
        chosen = (bg == g) & ((i1 == i) | (i2 == i))
        gates.append(jnp.where(chosen, scores[e] / denom, 0.0))
    return gates


def _merge_kernel(bias_ref, a_ref, u_ref, up_ref, un_ref, gl_ref, x_ref,
                  wpool_ref, pscale_ref, wa_ref, wp_ref, wo_ref, g2_ref, wrh_ref, wrl_ref,
                  x1_ref, h2_ref, gates_ref, ext_ref, gt_ref, *, seq_len):
    tm = u_ref.shape[1]
    i = pl.program_id(1)
    n_i = pl.num_programs(1)

    ext_ref[0:POOL_HALO, :] = jnp.where(i > 0, up_ref[0], 0.0)
    ext_ref[POOL_HALO:POOL_HALO + tm, :] = u_ref[0]
    ext_ref[POOL_HALO + tm:, :] = jnp.where(i < n_i - 1, un_ref[0], 0.0)
    t = i * tm + lax.broadcasted_iota(jnp.int32, (tm, 1), 0)
    mixed = []
    for g, w in enumerate(POOL_WINDOWS):
        left = w // 2
        right = w - 1 - left
        cols = slice(g * POOL_GROUP_DIM, (g + 1) * POOL_GROUP_DIM)
        acc = ext_ref[POOL_HALO - left:POOL_HALO - left + tm, cols]
        for d in range(-left + 1, right + 1):
            acc = acc + ext_ref[POOL_HALO + d:POOL_HALO + d + tm, cols]
        cnt = (jnp.minimum(t + right + 1, seq_len) - jnp.maximum(t - left, 0)).astype(_F32)
        pooled = acc / cnt - ext_ref[POOL_HALO:POOL_HALO + tm, cols]
        mixed.append(jnp.dot(pooled.astype(_BF16), wpool_ref[g], preferred_element_type=_F32))
    mixed = jnp.concatenate(mixed, axis=-1) * pscale_ref[...]

    a = jnp.dot(a_ref[0], wa_ref[...], preferred_element_type=_F32)
    p = jnp.dot(mixed.astype(_BF16), wp_ref[...], preferred_element_type=_F32)
    ga = _sigmoid(gl_ref[0, :, 0:D_MODEL].astype(_F32))
    gp = _sigmoid(gl_ref[0, :, D_MODEL:2 * D_MODEL].astype(_F32))
    m = ga * a + gp * p
    x1 = x_ref[0] + jnp.dot(m.astype(_BF16), wo_ref[...], preferred_element_type=_F32)
    x1_ref[0] = x1

    h2 = x1 * lax.rsqrt(jnp.mean(x1 * x1, axis=-1, keepdims=True) + EPS) * g2_ref[...]
    h2_hi = h2.astype(_BF16)
    h2_ref[0] = h2_hi

    h2_lo = (h2 - h2_hi.astype(_F32)).astype(_BF16)
    logits = (jnp.dot(h2_hi, wrh_ref[...], preferred_element_type=_F32)
              + jnp.dot(h2_lo, wrh_ref[...], preferred_element_type=_F32)
              + jnp.dot(h2_hi, wrl_ref[...], preferred_element_type=_F32))
    lt = logits.T
    gate_rows = _route_rows([lt[e:e + 1, :] for e in range(N_EXPERTS)], bias_ref)
    gt_ref[...] = jnp.zeros_like(gt_ref)
    for e in range(N_EXPERTS):
        gt_ref[e:e + 1, :] = gate_rows[e]
    gates_ref[0] = gt_ref[...].T[:, 0:N_EXPERTS]


def _merge_call(bias, a, u, gl, x, wpool, pscale, wa, wp, wo, g2, wrh, wrl, tm=256):
    b, s, _ = x.shape
    n_i = s // tm
    hb = tm // POOL_HALO
    n_hb = s // POOL_HALO
    tile = lambda w: pl.BlockSpec((1, tm, w), lambda bi, i: (bi, i, 0))
    return pl.pallas_call(
        functools.partial(_merge_kernel, seq_len=s),
        grid=(b, n_i),
        in_specs=[pl.BlockSpec(memory_space=pltpu.SMEM),
                  tile(ATTN_OUT_DIM), tile(POOL_DIM),
                  pl.BlockSpec((1, POOL_HALO, POOL_DIM),
                               lambda bi, i: (bi, jnp.maximum(i * hb - 1, 0), 0)),
                  pl.BlockSpec((1, POOL_HALO, POOL_DIM),
                               lambda bi, i: (bi, jnp.minimum((i + 1) * hb, n_hb - 1), 0)),
                  tile(2 * D_MODEL), tile(D_MODEL),
                  _const_spec(wpool.shape), _const_spec(pscale.shape), _const_spec(wa.shape),
                  _const_spec(wp.shape), _const_spec(wo.shape), _const_spec(g2.shape),
                  _const_spec(wrh.shape), _const_spec(wrl.shape)],
        out_specs=[tile(D_MODEL), tile(D_MODEL), tile(N_EXPERTS)],
        out_shape=[jax.ShapeDtypeStruct((b, s, D_MODEL), _F32),
                   jax.ShapeDtypeStruct((b, s, D_MODEL), _BF16),
                   jax.ShapeDtypeStruct((b, s, N_EXPERTS), _F32)],
        scratch_shapes=[pltpu.VMEM((tm + 2 * POOL_HALO, POOL_DIM), _F32),
                        pltpu.VMEM((LANES, tm), _F32)],
        compiler_params=pltpu.CompilerParams(dimension_semantics=("parallel", "parallel"),
                                             vmem_limit_bytes=VMEM_LIMIT),
        name="merge_route",
    )(bias, a, u, u, u, gl, x, wpool, pscale, wa, wp, wo, g2, wrh, wrl)


def _moe_kernel(h_ref, gates_ref, x_ref, wg_ref, wu_ref, wd_ref, o_ref):
    h = h_ref[...]
    y = x_ref[...]
    for e in range(N_EXPERTS):
        gate = jnp.dot(h, wg_ref[e], preferred_element_type=_F32)
        up = jnp.dot(h, wu_ref[e], preferred_element_type=_F32)
        act = gate * _sigmoid(gate) * up
        y = y + gates_ref[:, e:e + 1] * jnp.dot(act.astype(_BF16), wd_ref[e],
                                                preferred_element_type=_F32)
    o_ref[...] = y


def _moe_call(h2, gates, x1, wg, wu, wd, tm=512):
    n = h2.shape[0]
    row = lambda w: pl.BlockSpec((tm, w), lambda i: (i, 0))
    return pl.pallas_call(
        _moe_kernel,
        grid=(n // tm,),
        in_specs=[row(D_MODEL), row(N_EXPERTS), row(D_MODEL),
                  _const_spec(wg.shape), _const_spec(wu.shape), _const_spec(wd.shape)],
        out_specs=row(D_MODEL),
        out_shape=jax.ShapeDtypeStruct((n, D_MODEL), _F32),
        compiler_params=pltpu.CompilerParams(dimension_semantics=("parallel",),
                                             vmem_limit_bytes=VMEM_LIMIT),
        name="moe",
    )(h2, gates, x1, wg, wu, wd)


def _pad_heads(w, head_dim):
    lead = w.shape[:-1]
    w = w.reshape(*lead, N_HEADS, head_dim)
    w = jnp.pad(w, [(0, 0)] * len(lead) + [(0, 0), (0, HEAD_PAD - head_dim)])
    return w.reshape(*lead, N_HEADS * HEAD_PAD)


def _prep_layer(l, w_in, w_uq, w_ukv):
    q_end = Q_LORA_RANK
    kv_end = q_end + KV_LORA_RANK
    kr_end = kv_end + QK_ROPE_DIM
    u_end = kr_end + POOL_DIM
    wi = w_in[l]
    kr = jnp.pad(wi[:, kv_end:kr_end], ((0, 0), (QK_NOPE_DIM, HEAD_PAD - QK_HEAD_DIM)))
    win = jnp.concatenate([wi[:, :kv_end], kr, wi[:, kr_end:]], axis=1).astype(_BF16)
    assert win.shape[1] == D_IN_PAD and u_end + 2 * D_MODEL == wi.shape[1]
    wuq = _pad_heads(w_uq[l], QK_HEAD_DIM).astype(_BF16)
    wkv = w_ukv[l].reshape(KV_LORA_RANK, N_HEADS, QK_NOPE_DIM + V_HEAD_DIM)
    wuk = _pad_heads(wkv[:, :, :QK_NOPE_DIM].reshape(KV_LORA_RANK, -1), QK_NOPE_DIM).astype(_BF16)
    wuv = wkv[:, :, QK_NOPE_DIM:].reshape(KV_LORA_RANK, ATTN_OUT_DIM).astype(_BF16)
    return win, wuq, wuk, wuv


def kernel(x, positions, norm1_g, w_in, q_a_g, kv_a_g, w_uq, w_ukv, q_norm_g, k_norm_g,
           w_pool, pool_scale, w_branch_attn, w_branch_pool, w_out, norm2_g,
           w_router, router_bias, w_expert_gate, w_expert_up, w_expert_down):
    b, s, d = x.shape
    n = b * s
    depth = w_in.shape[0]

    inv_freq = 1.0 / (ROPE_THETA ** (jnp.arange(0, QK_ROPE_DIM, 2, dtype=_F32) / QK_ROPE_DIM))
    freq = jnp.zeros((1, HEAD_PAD), _F32)
    freq = freq.at[0, QK_NOPE_DIM:QK_NOPE_DIM + QK_ROPE_DIM // 2].set(inv_freq)
    freq = freq.at[0, QK_NOPE_DIM + QK_ROPE_DIM // 2:QK_HEAD_DIM].set(inv_freq)
    tabs = _rope_table_call(positions.astype(_F32).reshape(n, 1), freq)

    wr = jnp.pad(w_router.astype(_F32), ((0, 0), (0, LANES - N_EXPERTS)))
    wrh = wr.astype(_BF16)
    wrl = (wr - wrh.astype(_F32)).astype(_BF16)
    bias = router_bias.astype(_F32)

    xf = x.reshape(n, d)
    for l in range(depth):
        win, wuq, wuk, wuv = _prep_layer(l, w_in, w_uq, w_ukv)
        gq = jnp.pad(q_norm_g[l], (0, HEAD_PAD - QK_HEAD_DIM)).reshape(1, HEAD_PAD)
        gk = jnp.pad(k_norm_g[l], (0, HEAD_PAD - QK_HEAD_DIM)).reshape(1, HEAD_PAD)
        q, k, v, u, gl = _in_proj_call(
            xf, tabs, norm1_g[l].reshape(1, d), win, q_a_g[l].reshape(1, -1),
            kv_a_g[l].reshape(1, -1), wuq, wuk, wuv, gq, gk)
        a = _attention_call(q.reshape(b, s, -1), k.reshape(b, s, -1), v.reshape(b, s, -1))
        x1, h2, gates = _merge_call(
            bias, a, u.reshape(b, s, -1), gl.reshape(b, s, -1), xf.reshape(b, s, d),
            w_pool[l].astype(_BF16), pool_scale[l].reshape(1, -1),
            w_branch_attn[l].astype(_BF16), w_branch_pool[l].astype(_BF16),
            w_out[l].astype(_BF16), norm2_g[l].reshape(1, d), wrh, wrl)
        xf = _moe_call(h2.reshape(n, d), gates.reshape(n, N_EXPERTS), x1.reshape(n, d),
                       w_expert_gate[l].astype(_BF16), w_expert_up[l].astype(_BF16),
                       w_expert_down[l].astype(_BF16))
    return xf.reshape(b, s, d)
```

```python
import functools

import jax
import jax.numpy as jnp
from jax import lax
from jax.experimental import pallas as pl
from jax.experimental.pallas import tpu as pltpu

D_MODEL = 1024
N_HEADS = 8
QK_NOPE_DIM = 64
QK_ROPE_DIM = 32
QK_HEAD_DIM = QK_NOPE_DIM + QK_ROPE_DIM
V_HEAD_DIM = 64
Q_LORA_RANK = 384
KV_LORA_RANK = 256
ROPE_THETA = 10000.0
ATTN_SCALE = QK_HEAD_DIM ** -0.5
POOL_WINDOWS = (2, 4, 8, 16)
N_POOL_GROUPS = 4
POOL_DIM = D_MODEL // 2
POOL_GROUP_DIM = POOL_DIM // N_POOL_GROUPS
ATTN_OUT_DIM = N_HEADS * V_HEAD_DIM
N_EXPERTS = 16
N_EXPERT_GROUPS = 4
EXPERTS_PER_GROUP = N_EXPERTS // N_EXPERT_GROUPS
D_EXPERT = 256
EPS = 1e-6

LANES = 128
SUBLANES = 8
HEAD_PAD = LANES
QK_PAD_DIM = N_HEADS * HEAD_PAD
HEADS_PER_STEP = 2
POOL_HALO = 8
SHIFT_LANE = QK_HEAD_DIM
V_PAIR_PAD = 2 * LANES
V_PAD_DIM = (N_HEADS // HEADS_PER_STEP) * V_PAIR_PAD
LOG2E = 1.4426950408889634
SCORE_BOUND_LIMIT = 32.0

_C_Q = (0, Q_LORA_RANK)
_C_KV = (_C_Q[1], _C_Q[1] + KV_LORA_RANK)
_C_KR = (_C_KV[1], _C_KV[1] + HEAD_PAD)
_C_U = (_C_KR[1], _C_KR[1] + POOL_DIM)
_C_G = (_C_U[1], _C_U[1] + 2 * D_MODEL)
D_IN_PAD = _C_G[1]

VMEM_LIMIT = 56 * 1024 * 1024

_F32 = jnp.float32
_BF16 = jnp.bfloat16


def _const_spec(shape):
    nd = len(shape)
    return pl.BlockSpec(shape, lambda *_: (0,) * nd, pipeline_mode=pl.Buffered(1))


def _sigmoid(x):
    return 1.0 / (1.0 + jnp.exp(-x))


def _rope_table_kernel(pos_ref, freq_ref, c_ref, s1_ref, s2_ref):
    ang = pos_ref[...] * freq_ref[...]
    cos = jnp.cos(ang)
    sin = jnp.sin(ang)
    lane = lax.broadcasted_iota(jnp.int32, ang.shape, 1)
    half = QK_ROPE_DIM // 2
    first = (lane >= QK_NOPE_DIM) & (lane < QK_NOPE_DIM + half)
    second = (lane >= QK_NOPE_DIM + half) & (lane < QK_HEAD_DIM)
    c_ref[...] = jnp.where(first | second, cos, 1.0)
    s1_ref[...] = jnp.where(first, -sin, 0.0)
    s2_ref[...] = jnp.where(second, sin, 0.0)


def _rope_table_call(pos, freq, tm=1024):
    n = pos.shape[0]
    tab = jax.ShapeDtypeStruct((n, HEAD_PAD), _F32)
    return pl.pallas_call(
        _rope_table_kernel,
        grid=(n // tm,),
        in_specs=[pl.BlockSpec((tm, 1), lambda i: (i, 0)), _const_spec((1, HEAD_PAD))],
        out_specs=[pl.BlockSpec((tm, HEAD_PAD), lambda i: (i, 0))] * 3,
        out_shape=[tab, tab, tab],
        compiler_params=pltpu.CompilerParams(dimension_semantics=("parallel",)),
        name="rope_table",
    )(pos, freq)


def _rope(x, c, s1, s2):
    half = QK_ROPE_DIM // 2
    return (x * c + pltpu.roll(x, HEAD_PAD - half, 1) * s1 + pltpu.roll(x, half, 1) * s2)


def _in_proj_kernel(x_ref, c_ref, s1_ref, s2_ref, g1_ref, win_ref, qag_ref, kvag_ref,
                    wuq_ref, wuk_ref, wuv_ref, gq_ref, gk_ref, qpad_ref, kpad_ref, vpad_ref,
                    q_ref, k_ref, v_ref, u_ref, gl_ref):
    x = x_ref[...]
    h = x * lax.rsqrt(jnp.mean(x * x, axis=-1, keepdims=True) + EPS) * g1_ref[...]
    h = h.astype(_BF16)

    def proj(cols):
        return jnp.dot(h, win_ref[:, cols[0]:cols[1]], preferred_element_type=_F32)

    gl_ref[...] = proj(_C_G).astype(_BF16)
    u_ref[...] = proj(_C_U)

    c, s1, s2 = c_ref[...], s1_ref[...], s2_ref[...]

    cq = proj(_C_Q)
    cqn = cq * lax.rsqrt(jnp.mean(cq * cq, axis=-1, keepdims=True) + EPS) * qag_ref[...]
    qp = jnp.dot(cqn.astype(_BF16), wuq_ref[...], preferred_element_type=_F32)
    gq = gq_ref[...] * (ATTN_SCALE * LOG2E)
    qpad = qpad_ref[...]
    for hd in range(N_HEADS):
        sl = slice(hd * HEAD_PAD, (hd + 1) * HEAD_PAD)
        qh = qp[:, sl]
        r = lax.rsqrt(jnp.sum(qh * qh, axis=-1, keepdims=True) / QK_HEAD_DIM + EPS)
        q_ref[:, sl] = (_rope(qh * r * gq, c, s1, s2) + qpad).astype(_BF16)

    ckv = proj(_C_KV)
    ckvn = ckv * lax.rsqrt(jnp.mean(ckv * ckv, axis=-1, keepdims=True) + EPS) * kvag_ref[...]
    ckvn = ckvn.astype(_BF16)
    v = jnp.dot(ckvn, wuv_ref[...], preferred_element_type=_F32) + vpad_ref[...]
    v_ref[...] = v.astype(_BF16)
    kp = jnp.dot(ckvn, wuk_ref[...], preferred_element_type=_F32)
    kr = proj(_C_KR)
    gk = gk_ref[...]
    kpad = kpad_ref[...]
    for hd in range(N_HEADS):
        sl = slice(hd * HEAD_PAD, (hd + 1) * HEAD_PAD)
        kh = kp[:, sl] + kr
        r = lax.rsqrt(jnp.sum(kh * kh, axis=-1, keepdims=True) / QK_HEAD_DIM + EPS)
        k_ref[:, sl] = (_rope(kh * r * gk, c, s1, s2) + kpad).astype(_BF16)


def _in_proj_call(x, tabs, consts, tm=512):
    n = x.shape[0]
    row = lambda w: pl.BlockSpec((tm, w), lambda i: (i, 0))
    return pl.pallas_call(
        _in_proj_kernel,
        grid=(n // tm,),
        in_specs=[row(D_MODEL), row(HEAD_PAD), row(HEAD_PAD), row(HEAD_PAD)]
                 + [_const_spec(c.shape) for c in consts],
        out_specs=[row(QK_PAD_DIM), row(QK_PAD_DIM), row(V_PAD_DIM), row(POOL_DIM),
                   row(2 * D_MODEL)],
        out_shape=[jax.ShapeDtypeStruct((n, QK_PAD_DIM), _BF16),
                   jax.ShapeDtypeStruct((n, QK_PAD_DIM), _BF16),
                   jax.ShapeDtypeStruct((n, V_PAD_DIM), _BF16),
                   jax.ShapeDtypeStruct((n, POOL_DIM), _F32),
                   jax.ShapeDtypeStruct((n, 2 * D_MODEL), _BF16)],
        compiler_params=pltpu.CompilerParams(dimension_semantics=("parallel",),
                                             vmem_limit_bytes=VMEM_LIMIT),
        name="in_proj",
    )(x, *tabs, *consts)


_NT_DIMS = (((1,), (1,)), ((), ()))


def _pick_heads(outs):
    lane = lax.broadcasted_iota(jnp.int32, outs[0].shape, 1)
    return jnp.where(lane < V_HEAD_DIM, outs[0], outs[1])


def _attention_shifted_kernel(q_ref, k_ref, v_ref, o_ref, *, tk):
    n_kv = k_ref.shape[1] // tk
    outs = []
    for hd in range(HEADS_PER_STEP):
        sl = slice(hd * HEAD_PAD, (hd + 1) * HEAD_PAD)
        q = q_ref[0, :, sl]
        acc = None
        for c in range(n_kv):
            rows = slice(c * tk, (c + 1) * tk)
            s = lax.dot_general(q, k_ref[0, rows, sl], _NT_DIMS, preferred_element_type=_F32)
            o = jnp.dot(jnp.exp2(s).astype(_BF16), v_ref[0, rows, :], preferred_element_type=_F32)
            acc = o if acc is None else acc + o
        outs.append(acc[:, 0:LANES] / acc[:, LANES:LANES + 1])
    o_ref[0] = _pick_heads(outs).astype(o_ref.dtype)


def _attention_online_kernel(q_ref, k_ref, v_ref, o_ref, *, tk):
    tq = q_ref.shape[1]
    n_kv = k_ref.shape[1] // tk
    outs = []
    for hd in range(HEADS_PER_STEP):
        sl = slice(hd * HEAD_PAD, (hd + 1) * HEAD_PAD)
        q = q_ref[0, :, sl]

        def body(j, carry, sl=sl, q=q):
            m, l, acc = carry
            rows = pl.ds(pl.multiple_of(j * tk, tk), tk)
            s = lax.dot_general(q, k_ref[0, rows, sl], _NT_DIMS, preferred_element_type=_F32)
            m_new = jnp.maximum(m, jnp.max(s, axis=-1, keepdims=True))
            alpha = jnp.exp2(m - m_new)
            p = jnp.exp2(s - m_new)
            l = alpha * l + jnp.sum(p, axis=-1, keepdims=True)
            acc = alpha * acc + jnp.dot(p.astype(_BF16), v_ref[0, rows, 0:LANES],
                                        preferred_element_type=_F32)
            return m_new, l, acc

        init = (jnp.full((tq, 1), -jnp.inf, _F32), jnp.zeros((tq, 1), _F32),
                jnp.zeros((tq, LANES), _F32))
        _, l, acc = lax.fori_loop(0, n_kv, body, init)
        outs.append(acc / l)
    o_ref[0] = _pick_heads(outs).astype(o_ref.dtype)


def _attention_call(body, q, k, v, tq, tk, name):
    b, s, _ = q.shape
    qw = HEADS_PER_STEP * HEAD_PAD
    ow = HEADS_PER_STEP * V_HEAD_DIM
    return pl.pallas_call(
        functools.partial(body, tk=tk),
        grid=(b, N_HEADS // HEADS_PER_STEP, s // tq),
        in_specs=[pl.BlockSpec((1, tq, qw), lambda bi, hp, qi: (bi, qi, hp)),
                  pl.BlockSpec((1, s, qw), lambda bi, hp, qi: (bi, 0, hp)),
                  pl.BlockSpec((1, s, V_PAIR_PAD), lambda bi, hp, qi: (bi, 0, hp))],
        out_specs=pl.BlockSpec((1, tq, ow), lambda bi, hp, qi: (bi, qi, hp)),
        out_shape=jax.ShapeDtypeStruct((b, s, ATTN_OUT_DIM), _BF16),
        compiler_params=pltpu.CompilerParams(
            dimension_semantics=("parallel", "parallel", "arbitrary"),
            vmem_limit_bytes=VMEM_LIMIT),
        name=name,
    )(q, k, v)


def _attention(q, k, v, use_shift):
    shifted = functools.partial(_attention_call, _attention_shifted_kernel,
                                tq=512, tk=512, name="attention_shifted")
    online = functools.partial(_attention_call, _attention_online_kernel,
                               tq=512, tk=512, name="attention_online")
    return lax.cond(use_shift, shifted, online, q, k, v)


def _route_rows(logit_rows, bias_ref):
    scores = [_sigmoid(r) for r in logit_rows]
    biased = [scores[e] + bias_ref[e] for e in range(N_EXPERTS)]
    gscore = []
    for g in range(N_EXPERT_GROUPS):
        a, b, c, d = biased[g * EXPERTS_PER_GROUP:(g + 1) * EXPERTS_PER_GROUP]
        hi1, lo1 = jnp.maximum(a, b), jnp.minimum(a, b)
        hi2, lo2 = jnp.maximum(c, d), jnp.minimum(c, d)
        top = jnp.maximum(hi1, hi2)
        second = jnp.maximum(jnp.minimum(hi1, hi2), jnp.maximum(lo1, lo2))
        gscore.append(top + second)
    best = gscore[0]
    bg = jnp.zeros_like(best, dtype=jnp.int32)
    for g in range(1, N_EXPERT_GROUPS):
        better = gscore[g] > best
        best = jnp.where(better, gscore[g], best)
        bg = jnp.where(better, g, bg)
    vb, vs = [], []
    for i in range(EXPERTS_PER_GROUP):
        b_i, s_i = biased[i], scores[i]
        for g in range(1, N_EXPERT_GROUPS):
            pick = bg == g
            b_i = jnp.where(pick, biased[g * EXPERTS_PER_GROUP + i], b_i)
            s_i = jnp.where(pick, scores[g * EXPERTS_PER_GROUP + i], s_i)
        vb.append(b_i)
        vs.append(s_i)
    i1 = jnp.zeros_like(bg)
    b1 = vb[0]
    for i in range(1, EXPERTS_PER_GROUP):
        better = vb[i] > b1
        b1 = jnp.where(better, vb[i], b1)
        i1 = jnp.where(better, i, i1)
    i2 = jnp.full_like(bg, -1)
    b2 = jnp.full_like(b1, -jnp.inf)
    for i in range(EXPERTS_PER_GROUP):
        better = (i1 != i) & ((vb[i] > b2) | (i2 < 0))
        b2 = jnp.where(better, vb[i], b2)
        i2 = jnp.where(better, i, i2)
    s1 = vs[0]
    s2 = vs[0]
    for i in range(1, EXPERTS_PER_GROUP):
        s1 = jnp.where(i1 == i, vs[i], s1)
        s2 = jnp.where(i2 == i, vs[i], s2)
    denom = s1 + s2
    gates = []
    for e in range(N_EXPERTS):
        g, i = divmod(e, EXPERTS_PER_GROUP)
        chosen = (bg == g) & ((i1 == i) | (i2 == i))
        gates.append(jnp.where(chosen, scores[e] / denom, 0.0))
    return gates


def _merge_kernel(bias_ref, a_ref, u_ref, up_ref, un_ref, gl_ref, x_ref,
                  wpool_ref, pscale_ref, wa_ref, wp_ref, wo_ref, g2_ref, wrh_ref, wrl_ref,
                  x1_ref, h2_ref, gates_ref, ext_ref, gt_ref, *, seq_len):
    tm = u_ref.shape[1]
    i = pl.program_id(1)
    n_i = pl.num_programs(1)

    ext_ref[0:POOL_HALO, :] = jnp.where(i > 0, up_ref[0], 0.0)
    ext_ref[POOL_HALO:POOL_HALO + tm, :] = u_ref[0]
    ext_ref[POOL_HALO + tm:, :] = jnp.where(i < n_i - 1, un_ref[0], 0.0)
    t = i * tm + lax.broadcasted_iota(jnp.int32, (tm, 1), 0)
    mixed = []
    for g, w in enumerate(POOL_WINDOWS):
        left = w // 2
        right = w - 1 - left
        cols = slice(g * POOL_GROUP_DIM, (g + 1) * POOL_GROUP_DIM)
        acc = ext_ref[POOL_HALO - left:POOL_HALO - left + tm, cols]
        for d in range(-left + 1, right + 1):
            acc = acc + ext_ref[POOL_HALO + d:POOL_HALO + d + tm, cols]
        cnt = (jnp.minimum(t + right + 1, seq_len) - jnp.maximum(t - left, 0)).astype(_F32)
        pooled = acc / cnt - ext_ref[POOL_HALO:POOL_HALO + tm, cols]
        mixed.append(jnp.dot(pooled.astype(_BF16), wpool_ref[g], preferred_element_type=_F32))
    mixed = jnp.concatenate(mixed, axis=-1) * pscale_ref[...]

    a = jnp.dot(a_ref[0], wa_ref[...], preferred_element_type=_F32)
    p = jnp.dot(mixed.astype(_BF16), wp_ref[...], preferred_element_type=_F32)
    ga = _sigmoid(gl_ref[0, :, 0:D_MODEL].astype(_F32))
    gp = _sigmoid(gl_ref[0, :, D_MODEL:2 * D_MODEL].astype(_F32))
    m = ga * a + gp * p
    x1 = x_ref[0] + jnp.dot(m.astype(_BF16), wo_ref[...], preferred_element_type=_F32)
    x1_ref[0] = x1

    h2 = x1 * lax.rsqrt(jnp.mean(x1 * x1, axis=-1, keepdims=True) + EPS) * g2_ref[...]
    h2_hi = h2.astype(_BF16)
    h2_ref[0] = h2_hi

    h2_lo = (h2 - h2_hi.astype(_F32)).astype(_BF16)
    logits = (jnp.dot(h2_hi, wrh_ref[...], preferred_element_type=_F32)
              + jnp.dot(h2_lo, wrh_ref[...], preferred_element_type=_F32)
              + jnp.dot(h2_hi, wrl_ref[...], preferred_element_type=_F32))
    lt = logits.T
    gate_rows = _route_rows([lt[e:e + 1, :] for e in range(N_EXPERTS)], bias_ref)
    gt_ref[...] = jnp.zeros_like(gt_ref)
    for e in range(N_EXPERTS):
        gt_ref[e:e + 1, :] = gate_rows[e]
    gates_ref[0] = gt_ref[...].T[:, 0:N_EXPERTS]


def _merge_call(bias, a, u, gl, x, wpool, pscale, wa, wp, wo, g2, wrh, wrl, tm=256):
    b, s, _ = x.shape
    n_i = s // tm
    hb = tm // POOL_HALO
    n_hb = s // POOL_HALO
    tile = lambda w: pl.BlockSpec((1, tm, w), lambda bi, i: (bi, i, 0))
    return pl.pallas_call(
        functools.partial(_merge_kernel, seq_len=s),
        grid=(b, n_i),
        in_specs=[pl.BlockSpec(memory_space=pltpu.SMEM),
                  tile(ATTN_OUT_DIM), tile(POOL_DIM),
                  pl.BlockSpec((1, POOL_HALO, POOL_DIM),
                               lambda bi, i: (bi, jnp.maximum(i * hb - 1, 0), 0)),
                  pl.BlockSpec((1, POOL_HALO, POOL_DIM),
                               lambda bi, i: (bi, jnp.minimum((i + 1) * hb, n_hb - 1), 0)),
                  tile(2 * D_MODEL), tile(D_MODEL),
                  _const_spec(wpool.shape), _const_spec(pscale.shape), _const_spec(wa.shape),
                  _const_spec(wp.shape), _const_spec(wo.shape), _const_spec(g2.shape),
                  _const_spec(wrh.shape), _const_spec(wrl.shape)],
        out_specs=[tile(D_MODEL), tile(D_MODEL), tile(N_EXPERTS)],
        out_shape=[jax.ShapeDtypeStruct((b, s, D_MODEL), _F32),
                   jax.ShapeDtypeStruct((b, s, D_MODEL), _BF16),
                   jax.ShapeDtypeStruct((b, s, N_EXPERTS), _F32)],
        scratch_shapes=[pltpu.VMEM((tm + 2 * POOL_HALO, POOL_DIM), _F32),
                        pltpu.VMEM((LANES, tm), _F32)],
        compiler_params=pltpu.CompilerParams(dimension_semantics=("parallel", "parallel"),
                                             vmem_limit_bytes=VMEM_LIMIT),
        name="merge_route",
    )(bias, a, u, u, u, gl, x, wpool, pscale, wa, wp, wo, g2, wrh, wrl)


def _moe_kernel(h_ref, gates_ref, x_ref, wg_ref, wu_ref, wd_ref, o_ref):
    h = h_ref[...]
    y = x_ref[...]
    for e in range(N_EXPERTS):
        gate = jnp.dot(h, wg_ref[e], preferred_element_type=_F32)
        up = jnp.dot(h, wu_ref[e], preferred_element_type=_F32)
        act = gate * _sigmoid(gate) * up
        y = y + gates_ref[:, e:e + 1] * jnp.dot(act.astype(_BF16), wd_ref[e],
                                                preferred_element_type=_F32)
    o_ref[...] = y


def _moe_call(h2, gates, x1, wg, wu, wd, tm=512):
    n = h2.shape[0]
    row = lambda w: pl.BlockSpec((tm, w), lambda i: (i, 0))
    return pl.pallas_call(
        _moe_kernel,
        grid=(n // tm,),
        in_specs=[row(D_MODEL), row(N_EXPERTS), row(D_MODEL),
                  _const_spec(wg.shape), _const_spec(wu.shape), _const_spec(wd.shape)],
        out_specs=row(D_MODEL),
        out_shape=jax.ShapeDtypeStruct((n, D_MODEL), _F32),
        compiler_params=pltpu.CompilerParams(dimension_semantics=("parallel",),
                                             vmem_limit_bytes=VMEM_LIMIT),
        name="moe",
    )(h2, gates, x1, wg, wu, wd)


def _pad_heads(w, head_dim):
    lead = w.shape[:-1]
    w = w.reshape(*lead, N_HEADS, head_dim)
    w = jnp.pad(w, [(0, 0)] * len(lead) + [(0, 0), (0, HEAD_PAD - head_dim)])
    return w.reshape(*lead, N_HEADS * HEAD_PAD)


def _prep_layer(l, w_in, w_uq, w_ukv):
    q_end = Q_LORA_RANK
    kv_end = q_end + KV_LORA_RANK
    kr_end = kv_end + QK_ROPE_DIM
    u_end = kr_end + POOL_DIM
    wi = w_in[l]
    kr = jnp.pad(wi[:, kv_end:kr_end], ((0, 0), (QK_NOPE_DIM, HEAD_PAD - QK_HEAD_DIM)))
    win = jnp.concatenate([wi[:, :kv_end], kr, wi[:, kr_end:]], axis=1).astype(_BF16)
    assert win.shape[1] == D_IN_PAD and u_end + 2 * D_MODEL == wi.shape[1]
    wuq = _pad_heads(w_uq[l], QK_HEAD_DIM).astype(_BF16)
    wkv = w_ukv[l].reshape(KV_LORA_RANK, N_HEADS, QK_NOPE_DIM + V_HEAD_DIM)
    wuk = _pad_heads(wkv[:, :, :QK_NOPE_DIM].reshape(KV_LORA_RANK, -1), QK_NOPE_DIM).astype(_BF16)
    wuv = wkv[:, :, QK_NOPE_DIM:].reshape(KV_LORA_RANK, N_HEADS // HEADS_PER_STEP, LANES)
    wuv = jnp.pad(wuv, ((0, 0), (0, 0), (0, V_PAIR_PAD - LANES)))
    wuv = wuv.reshape(KV_LORA_RANK, V_PAD_DIM).astype(_BF16)
    return win, wuq, wuk, wuv


def kernel(x, positions, norm1_g, w_in, q_a_g, kv_a_g, w_uq, w_ukv, q_norm_g, k_norm_g,
           w_pool, pool_scale, w_branch_attn, w_branch_pool, w_out, norm2_g,
           w_router, router_bias, w_expert_gate, w_expert_up, w_expert_down):
    b, s, d = x.shape
    n = b * s
    depth = w_in.shape[0]

    inv_freq = 1.0 / (ROPE_THETA ** (jnp.arange(0, QK_ROPE_DIM, 2, dtype=_F32) / QK_ROPE_DIM))
    freq = jnp.zeros((1, HEAD_PAD), _F32)
    freq = freq.at[0, QK_NOPE_DIM:QK_NOPE_DIM + QK_ROPE_DIM // 2].set(inv_freq)
    freq = freq.at[0, QK_NOPE_DIM + QK_ROPE_DIM // 2:QK_HEAD_DIM].set(inv_freq)
    tabs = _rope_table_call(positions.astype(_F32).reshape(n, 1), freq)

    wr = jnp.pad(w_router.astype(_F32), ((0, 0), (0, LANES - N_EXPERTS)))
    wrh = wr.astype(_BF16)
    wrl = (wr - wrh.astype(_F32)).astype(_BF16)
    bias = router_bias.astype(_F32)

    lane = jnp.arange(HEAD_PAD)
    qpad = (lane == SHIFT_LANE).astype(_F32).reshape(1, HEAD_PAD)
    vpad = jnp.tile((jnp.arange(V_PAIR_PAD) == LANES).astype(_F32), N_HEADS // HEADS_PER_STEP)
    vpad = vpad.reshape(1, V_PAD_DIM)

    xf = x.reshape(n, d)
    for l in range(depth):
        win, wuq, wuk, wuv = _prep_layer(l, w_in, w_uq, w_ukv)
        gq = jnp.pad(q_norm_g[l], (0, HEAD_PAD - QK_HEAD_DIM)).reshape(1, HEAD_PAD)
        gk = jnp.pad(k_norm_g[l], (0, HEAD_PAD - QK_HEAD_DIM)).reshape(1, HEAD_PAD)
        bound = (1.02 * QK_HEAD_DIM ** 0.5) * jnp.max(jnp.abs(gq)) * jnp.max(jnp.abs(gk))
        use_shift = bound <= SCORE_BOUND_LIMIT
        kpad = qpad * jnp.where(use_shift, -bound * LOG2E, 0.0)
        q, k, v, u, gl = _in_proj_call(
            xf, tabs, (norm1_g[l].reshape(1, d), win, q_a_g[l].reshape(1, -1),
                       kv_a_g[l].reshape(1, -1), wuq, wuk, wuv, gq, gk, qpad, kpad, vpad))
        a = _attention(q.reshape(b, s, -1), k.reshape(b, s, -1), v.reshape(b, s, -1), use_shift)
        x1, h2, gates = _merge_call(
            bias, a, u.reshape(b, s, -1), gl.reshape(b, s, -1), xf.reshape(b, s, d),
            w_pool[l].astype(_BF16), pool_scale[l].reshape(1, -1),
            w_branch_attn[l].astype(_BF16), w_branch_pool[l].astype(_BF16),
            w_out[l].astype(_BF16), norm2_g[l].reshape(1, d), wrh, wrl)
        xf = _moe_call(h2.reshape(n, d), gates.reshape(n, N_EXPERTS), x1.reshape(n, d),
                       w_expert_gate[l].astype(_BF16), w_expert_up[l].astype(_BF16),
                       w_expert_down[l].astype(_BF16))
    return xf.reshape(b, s, d)
```

```python
import functools

import jax
import jax.numpy as jnp
from jax import lax
from jax.experimental import pallas as pl
from jax.experimental.pallas import tpu as pltpu

D_MODEL = 1024
N_HEADS = 8
QK_NOPE_DIM = 64
QK_ROPE_DIM = 32
QK_HEAD_DIM = QK_NOPE_DIM + QK_ROPE_DIM
V_HEAD_DIM = 64
Q_LORA_RANK = 384
KV_LORA_RANK = 256
ROPE_THETA = 10000.0
ATTN_SCALE = QK_HEAD_DIM ** -0.5
POOL_WINDOWS = (2, 4, 8, 16)
N_POOL_GROUPS = 4
POOL_DIM = D_MODEL // 2
POOL_GROUP_DIM = POOL_DIM // N_POOL_GROUPS
ATTN_OUT_DIM = N_HEADS * V_HEAD_DIM
N_EXPERTS = 16
N_EXPERT_GROUPS = 4
EXPERTS_PER_GROUP = N_EXPERTS // N_EXPERT_GROUPS
D_EXPERT = 256
EPS = 1e-6

LANES = 128
SUBLANES = 8
HEAD_PAD = LANES
QK_PAD_DIM = N_HEADS * HEAD_PAD
HEADS_PER_STEP = 2
POOL_HALO = 8
SHIFT_LANE = QK_HEAD_DIM
BF16_ROWS = 16
V_ROWS = 80
LOG2E = 1.4426950408889634
SCORE_BOUND_LIMIT = 32.0

_C_U = (0, POOL_DIM)
_C_G = (_C_U[1], _C_U[1] + 2 * D_MODEL)
_R_Q = (0, Q_LORA_RANK)
_R_KV = (_R_Q[1], _R_Q[1] + KV_LORA_RANK)
_R_KR = (_R_KV[1], _R_KV[1] + QK_ROPE_DIM)

VMEM_LIMIT = 56 * 1024 * 1024

_F32 = jnp.float32
_BF16 = jnp.bfloat16
_NT_DIMS = (((1,), (1,)), ((), ()))


def _const_spec(shape):
    nd = len(shape)
    return pl.BlockSpec(shape, lambda *_: (0,) * nd, pipeline_mode=pl.Buffered(1))


def _sigmoid(x):
    return 1.0 / (1.0 + jnp.exp(-x))


def _across(col_ref, n):
    return jnp.tile(col_ref[...], (1, n // LANES))


def _rope_table_kernel(pos_ref, freq_ref, cos_ref, sin_ref):
    ang = _across(freq_ref, pos_ref.shape[2]) * pos_ref[0]
    cos_ref[0] = jnp.cos(ang)
    sin_ref[0] = jnp.sin(ang)


def _rope_table_call(pos, freq):
    b, _, s = pos.shape
    half = QK_ROPE_DIM // 2
    tab = jax.ShapeDtypeStruct((b, half, s), _F32)
    return pl.pallas_call(
        _rope_table_kernel,
        grid=(b,),
        in_specs=[pl.BlockSpec((1, 1, s), lambda i: (i, 0, 0)), _const_spec(freq.shape)],
        out_specs=[pl.BlockSpec((1, half, s), lambda i: (i, 0, 0))] * 2,
        out_shape=[tab, tab],
        compiler_params=pltpu.CompilerParams(dimension_semantics=("parallel",)),
        name="rope_table",
    )(pos, freq)


def _rope_rows(x1, x2, cos, sin):
    return x1 * cos - x2 * sin, x1 * sin + x2 * cos


def _in_proj_kernel(x_ref, cos_ref, sin_ref, g1_ref, win_ref, wint_ref, qagt_ref, kvagt_ref,
                    wuqt_ref, wukt_ref, wuvt_ref, gqt_ref, gknt_ref, gkrt_ref, qpadt_ref,
                    kpadt_ref, vpadt_ref,
                    qt_ref, k_ref, vt_ref, u_ref, gl_ref):
    tm = x_ref.shape[1]
    half = QK_ROPE_DIM // 2
    x = x_ref[0]
    h = x * lax.rsqrt(jnp.mean(x * x, axis=-1, keepdims=True) + EPS) * g1_ref[...]
    h = h.astype(_BF16)
    cos, sin = cos_ref[0], sin_ref[0]

    zt = lax.dot_general(wint_ref[...], h, _NT_DIMS, preferred_element_type=_F32)
    cqt = zt[_R_Q[0]:_R_Q[1]]
    cqt = cqt * lax.rsqrt(jnp.mean(cqt * cqt, axis=0, keepdims=True) + EPS) * _across(qagt_ref, tm)
    qt = jnp.dot(wuqt_ref[...], cqt.astype(_BF16), preferred_element_type=_F32)
    gqt = _across(gqt_ref, tm)
    qpadt = _across(qpadt_ref, tm)
    for hd in range(N_HEADS):
        qh = qt[hd * HEAD_PAD:(hd + 1) * HEAD_PAD]
        r = lax.rsqrt(jnp.sum(qh * qh, axis=0, keepdims=True) / QK_HEAD_DIM + EPS)
        qh = qh * r * gqt
        o1, o2 = _rope_rows(qh[QK_NOPE_DIM:QK_NOPE_DIM + half],
                            qh[QK_NOPE_DIM + half:QK_HEAD_DIM], cos, sin)
        qh = jnp.concatenate([qh[0:QK_NOPE_DIM], o1, o2, qh[QK_HEAD_DIM:]], axis=0) + qpadt
        qt_ref[0, hd * HEAD_PAD:(hd + 1) * HEAD_PAD, :] = qh.astype(_BF16)

    ckvt = zt[_R_KV[0]:_R_KV[1]]
    ckvt = (ckvt * lax.rsqrt(jnp.mean(ckvt * ckvt, axis=0, keepdims=True) + EPS)
            * _across(kvagt_ref, tm))
    ckvt = ckvt.astype(_BF16)
    vt = jnp.dot(wuvt_ref[...], ckvt, preferred_element_type=_F32)
    vt_ref[0] = (vt + _across(vpadt_ref, tm)).astype(_BF16)

    krt = zt[_R_KR[0]:_R_KR[1]]
    kr_ss = jnp.sum(krt * krt, axis=0, keepdims=True)
    krt = krt * _across(gkrt_ref, tm)
    kr1, kr2 = _rope_rows(krt[0:half], krt[half:QK_ROPE_DIM], cos, sin)
    knt = jnp.dot(wukt_ref[...], ckvt, preferred_element_type=_F32)
    gknt = _across(gknt_ref, tm)
    kpadt = _across(kpadt_ref, tm)
    for hd in range(N_HEADS):
        kn = knt[hd * QK_NOPE_DIM:(hd + 1) * QK_NOPE_DIM]
        ss = jnp.sum(kn * kn, axis=0, keepdims=True) + kr_ss
        r = lax.rsqrt(ss / QK_HEAD_DIM + EPS)
        kh = jnp.concatenate([kn * gknt * r, kr1 * r, kr2 * r, kpadt], axis=0)
        k_ref[0, :, hd * HEAD_PAD:(hd + 1) * HEAD_PAD] = kh.T.astype(_BF16)

    gl_ref[0] = jnp.dot(h, win_ref[:, _C_G[0]:_C_G[1]], preferred_element_type=_F32).astype(_BF16)
    u_ref[0] = jnp.dot(h, win_ref[:, _C_U[0]:_C_U[1]], preferred_element_type=_F32)


def _in_proj_call(x, tabs, consts, tm=512):
    b, s, _ = x.shape
    half = QK_ROPE_DIM // 2
    row = lambda w: pl.BlockSpec((1, tm, w), lambda bi, i: (bi, i, 0))
    col = lambda r: pl.BlockSpec((1, r, tm), lambda bi, i: (bi, 0, i))
    return pl.pallas_call(
        _in_proj_kernel,
        grid=(b, s // tm),
        in_specs=[row(D_MODEL), col(half), col(half)] + [_const_spec(c.shape) for c in consts],
        out_specs=[col(QK_PAD_DIM), row(QK_PAD_DIM), col(N_HEADS * V_ROWS), row(POOL_DIM),
                   row(2 * D_MODEL)],
        out_shape=[jax.ShapeDtypeStruct((b, QK_PAD_DIM, s), _BF16),
                   jax.ShapeDtypeStruct((b, s, QK_PAD_DIM), _BF16),
                   jax.ShapeDtypeStruct((b, N_HEADS * V_ROWS, s), _BF16),
                   jax.ShapeDtypeStruct((b, s, POOL_DIM), _F32),
                   jax.ShapeDtypeStruct((b, s, 2 * D_MODEL), _BF16)],
        compiler_params=pltpu.CompilerParams(dimension_semantics=("parallel", "parallel"),
                                             vmem_limit_bytes=VMEM_LIMIT),
        name="in_proj",
    )(x, *tabs, *consts)


def _finish_heads(accs, o_ref):
    ot = jnp.concatenate([a[0:V_HEAD_DIM] / a[V_HEAD_DIM:V_HEAD_DIM + 1] for a in accs], axis=0)
    o_ref[0] = ot.T.astype(o_ref.dtype)


def _attention_shifted_kernel(qt_ref, k_ref, vt_ref, o_ref, *, tk):
    n_kv = k_ref.shape[1] // tk
    chunks = [(hd, c) for hd in range(HEADS_PER_STEP) for c in range(n_kv)]
    qts = [qt_ref[0, hd * HEAD_PAD:(hd + 1) * HEAD_PAD, :] for hd in range(HEADS_PER_STEP)]

    def scores(hd, c):
        return jnp.dot(k_ref[0, c * tk:(c + 1) * tk, hd * HEAD_PAD:(hd + 1) * HEAD_PAD], qts[hd],
                       preferred_element_type=_F32)

    accs = [None] * HEADS_PER_STEP
    st_next = scores(*chunks[0])
    for i, (hd, c) in enumerate(chunks):
        st = st_next
        if i + 1 < len(chunks):
            st_next = scores(*chunks[i + 1])
        o = jnp.dot(vt_ref[0, hd * V_ROWS:(hd + 1) * V_ROWS, c * tk:(c + 1) * tk],
                    jnp.exp2(st).astype(_BF16), preferred_element_type=_F32)
        accs[hd] = o if accs[hd] is None else accs[hd] + o
    _finish_heads(accs, o_ref)


def _attention_online_kernel(qt_ref, k_ref, vt_ref, o_ref, *, tk):
    tq = qt_ref.shape[2]
    n_kv = k_ref.shape[1] // tk
    accs = []
    for hd in range(HEADS_PER_STEP):
        qt = qt_ref[0, hd * HEAD_PAD:(hd + 1) * HEAD_PAD, :]
        m = jnp.full((1, tq), -jnp.inf, _F32)
        acc = jnp.zeros((V_ROWS, tq), _F32)
        for c in range(n_kv):
            keys = slice(c * tk, (c + 1) * tk)
            st = jnp.dot(k_ref[0, keys, hd * HEAD_PAD:(hd + 1) * HEAD_PAD], qt,
                         preferred_element_type=_F32)
            m_new = jnp.maximum(m, jnp.max(st, axis=0, keepdims=True))
            pt = jnp.exp2(st - m_new).astype(_BF16)
            acc = jnp.exp2(m - m_new) * acc + jnp.dot(
                vt_ref[0, hd * V_ROWS:(hd + 1) * V_ROWS, keys], pt, preferred_element_type=_F32)
            m = m_new
        accs.append(acc)
    _finish_heads(accs, o_ref)


def _attention_call(body, qt, k, vt, tq, tk, name):
    b, s, _ = k.shape
    qw = HEADS_PER_STEP * HEAD_PAD
    vw = HEADS_PER_STEP * V_ROWS
    ow = HEADS_PER_STEP * V_HEAD_DIM
    return pl.pallas_call(
        functools.partial(body, tk=tk),
        grid=(b, N_HEADS // HEADS_PER_STEP, s // tq),
        in_specs=[pl.BlockSpec((1, qw, tq), lambda bi, hp, qi: (bi, hp, qi)),
                  pl.BlockSpec((1, s, qw), lambda bi, hp, qi: (bi, 0, hp)),
                  pl.BlockSpec((1, vw, s), lambda bi, hp, qi: (bi, hp, 0))],
        out_specs=pl.BlockSpec((1, tq, ow), lambda bi, hp, qi: (bi, qi, hp)),
        out_shape=jax.ShapeDtypeStruct((b, s, ATTN_OUT_DIM), _BF16),
        compiler_params=pltpu.CompilerParams(
            dimension_semantics=("parallel", "parallel", "arbitrary"),
            vmem_limit_bytes=VMEM_LIMIT),
        name=name,
    )(qt, k, vt)


def _attention(qt, k, vt, use_shift):
    shifted = functools.partial(_attention_call, _attention_shifted_kernel,
                                tq=1024, tk=256, name="attention_shifted")
    online = functools.partial(_attention_call, _attention_online_kernel,
                               tq=512, tk=512, name="attention_online")
    return lax.cond(use_shift, shifted, online, qt, k, vt)


def _route_rows(logit_rows, bias_ref):
    scores = [_sigmoid(r) for r in logit_rows]
    biased = [scores[e] + bias_ref[e] for e in range(N_EXPERTS)]
    gscore = []
    for g in range(N_EXPERT_GROUPS):
        a, b, c, d = biased[g * EXPERTS_PER_GROUP:(g + 1) * EXPERTS_PER_GROUP]
        hi1, lo1 = jnp.maximum(a, b), jnp.minimum(a, b)
        hi2, lo2 = jnp.maximum(c, d), jnp.minimum(c, d)
        top = jnp.maximum(hi1, hi2)
        second = jnp.maximum(jnp.minimum(hi1, hi2), jnp.maximum(lo1, lo2))
        gscore.append(top + second)
    best = gscore[0]
    bg = jnp.zeros_like(best, dtype=jnp.int32)
    for g in range(1, N_EXPERT_GROUPS):
        better = gscore[g] > best
        best = jnp.where(better, gscore[g], best)
        bg = jnp.where(better, g, bg)
    vb, vs = [], []
    for i in range(EXPERTS_PER_GROUP):
        b_i, s_i = biased[i], scores[i]
        for g in range(1, N_EXPERT_GROUPS):
            pick = bg == g
            b_i = jnp.where(pick, biased[g * EXPERTS_PER_GROUP + i], b_i)
            s_i = jnp.where(pick, scores[g * EXPERTS_PER_GROUP + i], s_i)
        vb.append(b_i)
        vs.append(s_i)
    i1 = jnp.zeros_like(bg)
    b1 = vb[0]
    for i in range(1, EXPERTS_PER_GROUP):
        better = vb[i] > b1
        b1 = jnp.where(better, vb[i], b1)
        i1 = jnp.where(better, i, i1)
    i2 = jnp.full_like(bg, -1)
    b2 = jnp.full_like(b1, -jnp.inf)
    for i in range(EXPERTS_PER_GROUP):
        better = (i1 != i) & ((vb[i] > b2) | (i2 < 0))
        b2 = jnp.where(better, vb[i], b2)
        i2 = jnp.where(better, i, i2)
    s1 = vs[0]
    s2 = vs[0]
    for i in range(1, EXPERTS_PER_GROUP):
        s1 = jnp.where(i1 == i, vs[i], s1)
        s2 = jnp.where(i2 == i, vs[i], s2)
    denom = s1 + s2
    gates = []
    for e in range(N_EXPERTS):
        g, i = divmod(e, EXPERTS_PER_GROUP)
        chosen = (bg == g) & ((i1 == i) | (i2 == i))
        gates.append(jnp.where(chosen, scores[e] / denom, 0.0))
    return gates


def _merge_kernel(bias_ref, a_ref, u_ref, up_ref, un_ref, gl_ref, x_ref,
                  wpool_ref, pscale_ref, wa_ref, wp_ref, wo_ref, g2_ref, wrh_ref, wrl_ref,
                  x1_ref, h2_ref, gates_ref, ext_ref, gt_ref, *, seq_len):
    tm = u_ref.shape[1]
    i = pl.program_id(1)
    n_i = pl.num_programs(1)

    ext_ref[0:POOL_HALO, :] = jnp.where(i > 0, up_ref[0], 0.0)
    ext_ref[POOL_HALO:POOL_HALO + tm, :] = u_ref[0]
    ext_ref[POOL_HALO + tm:, :] = jnp.where(i < n_i - 1, un_ref[0], 0.0)
    t = i * tm + lax.broadcasted_iota(jnp.int32, (tm, 1), 0)
    mixed = []
    for g, w in enumerate(POOL_WINDOWS):
        left = w // 2
        right = w - 1 - left
        cols = slice(g * POOL_GROUP_DIM, (g + 1) * POOL_GROUP_DIM)
        acc = ext_ref[POOL_HALO - left:POOL_HALO - left + tm, cols]
        for d in range(-left + 1, right + 1):
            acc = acc + ext_ref[POOL_HALO + d:POOL_HALO + d + tm, cols]
        cnt = (jnp.minimum(t + right + 1, seq_len) - jnp.maximum(t - left, 0)).astype(_F32)
        pooled = acc / cnt - ext_ref[POOL_HALO:POOL_HALO + tm, cols]
        mixed.append(jnp.dot(pooled.astype(_BF16), wpool_ref[g], preferred_element_type=_F32))
    mixed = jnp.concatenate(mixed, axis=-1) * pscale_ref[...]

    a = jnp.dot(a_ref[0], wa_ref[...], preferred_element_type=_F32)
    p = jnp.dot(mixed.astype(_BF16), wp_ref[...], preferred_element_type=_F32)
    ga = _sigmoid(gl_ref[0, :, 0:D_MODEL].astype(_F32))
    gp = _sigmoid(gl_ref[0, :, D_MODEL:2 * D_MODEL].astype(_F32))
    m = ga * a + gp * p
    x1 = x_ref[0] + jnp.dot(m.astype(_BF16), wo_ref[...], preferred_element_type=_F32)
    x1_ref[0] = x1

    h2 = x1 * lax.rsqrt(jnp.mean(x1 * x1, axis=-1, keepdims=True) + EPS) * g2_ref[...]
    h2_hi = h2.astype(_BF16)
    h2_ref[0] = h2_hi

    h2_lo = (h2 - h2_hi.astype(_F32)).astype(_BF16)
    logits = (jnp.dot(h2_hi, wrh_ref[...], preferred_element_type=_F32)
              + jnp.dot(h2_lo, wrh_ref[...], preferred_element_type=_F32)
              + jnp.dot(h2_hi, wrl_ref[...], preferred_element_type=_F32))
    lt = logits.T
    gate_rows = _route_rows([lt[e:e + 1, :] for e in range(N_EXPERTS)], bias_ref)
    gt_ref[...] = jnp.zeros_like(gt_ref)
    for e in range(N_EXPERTS):
        gt_ref[e:e + 1, :] = gate_rows[e]
    gates_ref[0] = gt_ref[...].T[:, 0:N_EXPERTS]


def _merge_call(bias, a, u, gl, x, wpool, pscale, wa, wp, wo, g2, wrh, wrl, tm=256):
    b, s, _ = x.shape
    n_i = s // tm
    hb = tm // POOL_HALO
    n_hb = s // POOL_HALO
    tile = lambda w: pl.BlockSpec((1, tm, w), lambda bi, i: (bi, i, 0))
    return pl.pallas_call(
        functools.partial(_merge_kernel, seq_len=s),
        grid=(b, n_i),
        in_specs=[pl.BlockSpec(memory_space=pltpu.SMEM),
                  tile(ATTN_OUT_DIM), tile(POOL_DIM),
                  pl.BlockSpec((1, POOL_HALO, POOL_DIM),
                               lambda bi, i: (bi, jnp.maximum(i * hb - 1, 0), 0)),
                  pl.BlockSpec((1, POOL_HALO, POOL_DIM),
                               lambda bi, i: (bi, jnp.minimum((i + 1) * hb, n_hb - 1), 0)),
                  tile(2 * D_MODEL), tile(D_MODEL),
                  _const_spec(wpool.shape), _const_spec(pscale.shape), _const_spec(wa.shape),
                  _const_spec(wp.shape), _const_spec(wo.shape), _const_spec(g2.shape),
                  _const_spec(wrh.shape), _const_spec(wrl.shape)],
        out_specs=[tile(D_MODEL), tile(D_MODEL), tile(N_EXPERTS)],
        out_shape=[jax.ShapeDtypeStruct((b, s, D_MODEL), _F32),
                   jax.ShapeDtypeStruct((b, s, D_MODEL), _BF16),
                   jax.ShapeDtypeStruct((b, s, N_EXPERTS), _F32)],
        scratch_shapes=[pltpu.VMEM((tm + 2 * POOL_HALO, POOL_DIM), _F32),
                        pltpu.VMEM((LANES, tm), _F32)],
        compiler_params=pltpu.CompilerParams(dimension_semantics=("parallel", "parallel"),
                                             vmem_limit_bytes=VMEM_LIMIT),
        name="merge_route",
    )(bias, a, u, u, u, gl, x, wpool, pscale, wa, wp, wo, g2, wrh, wrl)


def _moe_kernel(h_ref, gates_ref, x_ref, wg_ref, wu_ref, wd_ref, o_ref):
    h = h_ref[...]
    y = x_ref[...]
    for e in range(N_EXPERTS):
        gate = jnp.dot(h, wg_ref[e], preferred_element_type=_F32)
        up = jnp.dot(h, wu_ref[e], preferred_element_type=_F32)
        act = gate * _sigmoid(gate) * up
        y = y + gates_ref[:, e:e + 1] * jnp.dot(act.astype(_BF16), wd_ref[e],
                                                preferred_element_type=_F32)
    o_ref[...] = y


def _moe_call(h2, gates, x1, wg, wu, wd, tm=512):
    n = h2.shape[0]
    row = lambda w: pl.BlockSpec((tm, w), lambda i: (i, 0))
    return pl.pallas_call(
        _moe_kernel,
        grid=(n // tm,),
        in_specs=[row(D_MODEL), row(N_EXPERTS), row(D_MODEL),
                  _const_spec(wg.shape), _const_spec(wu.shape), _const_spec(wd.shape)],
        out_specs=row(D_MODEL),
        out_shape=jax.ShapeDtypeStruct((n, D_MODEL), _F32),
        compiler_params=pltpu.CompilerParams(dimension_semantics=("parallel",),
                                             vmem_limit_bytes=VMEM_LIMIT),
        name="moe",
    )(h2, gates, x1, wg, wu, wd)


def _pad_heads(w, head_dim):
    lead = w.shape[:-1]
    w = w.reshape(*lead, N_HEADS, head_dim)
    w = jnp.pad(w, [(0, 0)] * len(lead) + [(0, 0), (0, HEAD_PAD - head_dim)])
    return w.reshape(*lead, N_HEADS * HEAD_PAD)


def _rows(v):
    return jnp.broadcast_to(v.astype(_F32)[:, None], (v.shape[0], LANES))


def _prep_layer(l, w_in, w_uq, w_ukv, q_a_g, kv_a_g, q_norm_g, k_norm_g, norm1_g, kshift):
    q_end = Q_LORA_RANK
    kv_end = q_end + KV_LORA_RANK
    kr_end = kv_end + QK_ROPE_DIM
    wi = w_in[l]
    win = wi[:, kr_end:].astype(_BF16)
    wint = wi[:, :kr_end].T.astype(_BF16)
    wuqt = _pad_heads(w_uq[l], QK_HEAD_DIM).T.astype(_BF16)
    wkv = w_ukv[l].reshape(KV_LORA_RANK, N_HEADS, QK_NOPE_DIM + V_HEAD_DIM)
    wukt = wkv[:, :, :QK_NOPE_DIM].reshape(KV_LORA_RANK, -1).T.astype(_BF16)
    wuvt = jnp.pad(wkv[:, :, QK_NOPE_DIM:], ((0, 0), (0, 0), (0, V_ROWS - V_HEAD_DIM)))
    wuvt = wuvt.reshape(KV_LORA_RANK, N_HEADS * V_ROWS).T.astype(_BF16)

    pad = HEAD_PAD - QK_HEAD_DIM
    gq = jnp.pad(q_norm_g[l], (0, pad))
    gk = k_norm_g[l]
    shift_pos = (jnp.arange(HEAD_PAD) == SHIFT_LANE).astype(_F32)
    ones_row = (jnp.arange(V_ROWS) == V_HEAD_DIM).astype(_F32)
    return (norm1_g[l].reshape(1, -1), win, wint, _rows(q_a_g[l]), _rows(kv_a_g[l]),
            wuqt, wukt, wuvt,
            _rows(gq * (ATTN_SCALE * LOG2E)), _rows(gk[:QK_NOPE_DIM]), _rows(gk[QK_NOPE_DIM:]),
            _rows(shift_pos), _rows(shift_pos[QK_HEAD_DIM:] * kshift),
            _rows(jnp.tile(ones_row, N_HEADS)))


def kernel(x, positions, norm1_g, w_in, q_a_g, kv_a_g, w_uq, w_ukv, q_norm_g, k_norm_g,
           w_pool, pool_scale, w_branch_attn, w_branch_pool, w_out, norm2_g,
           w_router, router_bias, w_expert_gate, w_expert_up, w_expert_down):
    b, s, d = x.shape
    n = b * s
    depth = w_in.shape[0]

    inv_freq = 1.0 / (ROPE_THETA ** (jnp.arange(0, QK_ROPE_DIM, 2, dtype=_F32) / QK_ROPE_DIM))
    tabs = _rope_table_call(positions.astype(_F32).reshape(b, 1, s), _rows(inv_freq))

    wr = jnp.pad(w_router.astype(_F32), ((0, 0), (0, LANES - N_EXPERTS)))
    wrh = wr.astype(_BF16)
    wrl = (wr - wrh.astype(_F32)).astype(_BF16)
    bias = router_bias.astype(_F32)

    for l in range(depth):
        bound = ((1.02 * QK_HEAD_DIM ** 0.5) * jnp.max(jnp.abs(q_norm_g[l]))
                 * jnp.max(jnp.abs(k_norm_g[l])))
        use_shift = bound <= SCORE_BOUND_LIMIT
        kshift = jnp.where(use_shift, -bound * LOG2E, 0.0)
        consts = _prep_layer(l, w_in, w_uq, w_ukv, q_a_g, kv_a_g, q_norm_g, k_norm_g, norm1_g,
                             kshift)
        qt, k, vt, u, gl = _in_proj_call(x, tabs, consts)
        a = _attention(qt, k, vt, use_shift)
        x1, h2, gates = _merge_call(
            bias, a, u, gl, x,
            w_pool[l].astype(_BF16), pool_scale[l].reshape(1, -1),
            w_branch_attn[l].astype(_BF16), w_branch_pool[l].astype(_BF16),
            w_out[l].astype(_BF16), norm2_g[l].reshape(1, d), wrh, wrl)
        x = _moe_call(h2.reshape(n, d), gates.reshape(n, N_EXPERTS), x1.reshape(n, d),
                      w_expert_gate[l].astype(_BF16), w_expert_up[l].astype(_BF16),
                      w_expert_down[l].astype(_BF16)).reshape(b, s, d)
    return x
```

```python
import functools

import jax
import jax.numpy as jnp
from jax import lax
from jax.experimental import pallas as pl
from jax.experimental.pallas import tpu as pltpu

D_MODEL = 1024
N_HEADS = 8
QK_NOPE_DIM = 64
QK_ROPE_DIM = 32
QK_HEAD_DIM = QK_NOPE_DIM + QK_ROPE_DIM
V_HEAD_DIM = 64
Q_LORA_RANK = 384
KV_LORA_RANK = 256
ROPE_THETA = 10000.0
ATTN_SCALE = QK_HEAD_DIM ** -0.5
POOL_WINDOWS = (2, 4, 8, 16)
N_POOL_GROUPS = 4
POOL_DIM = D_MODEL // 2
POOL_GROUP_DIM = POOL_DIM // N_POOL_GROUPS
ATTN_OUT_DIM = N_HEADS * V_HEAD_DIM
N_EXPERTS = 16
N_EXPERT_GROUPS = 4
EXPERTS_PER_GROUP = N_EXPERTS // N_EXPERT_GROUPS
D_EXPERT = 256
EPS = 1e-6

LANES = 128
SUBLANES = 8
HEAD_PAD = LANES
QK_PAD_DIM = N_HEADS * HEAD_PAD
HEADS_PER_STEP = 2
POOL_HALO = 8
SHIFT_LANE = QK_HEAD_DIM
BF16_ROWS = 16
V_ROWS = 80
LOG2E = 1.4426950408889634
SCORE_BOUND_LIMIT = 32.0

_C_U = (0, POOL_DIM)
_C_G = (_C_U[1], _C_U[1] + 2 * D_MODEL)
_R_Q = (0, Q_LORA_RANK)
_R_KV = (_R_Q[1], _R_Q[1] + KV_LORA_RANK)
_R_KR = (_R_KV[1], _R_KV[1] + QK_ROPE_DIM)

VMEM_LIMIT = 56 * 1024 * 1024

_F32 = jnp.float32
_BF16 = jnp.bfloat16
_NT_DIMS = (((1,), (1,)), ((), ()))


def _const_spec(shape):
    nd = len(shape)
    return pl.BlockSpec(shape, lambda *_: (0,) * nd, pipeline_mode=pl.Buffered(1))


def _sigmoid(x):
    return 1.0 / (1.0 + jnp.exp2(x * (-LOG2E)))


def _across(col_ref, n):
    return jnp.tile(col_ref[...], (1, n // LANES))


def _rope_table_kernel(pos_ref, freq_ref, cos_ref, sin_ref):
    ang = _across(freq_ref, pos_ref.shape[2]) * pos_ref[0]
    cos_ref[0] = jnp.cos(ang)
    sin_ref[0] = jnp.sin(ang)


def _rope_table_call(pos, freq):
    b, _, s = pos.shape
    half = QK_ROPE_DIM // 2
    tab = jax.ShapeDtypeStruct((b, half, s), _F32)
    return pl.pallas_call(
        _rope_table_kernel,
        grid=(b,),
        in_specs=[pl.BlockSpec((1, 1, s), lambda i: (i, 0, 0)), _const_spec(freq.shape)],
        out_specs=[pl.BlockSpec((1, half, s), lambda i: (i, 0, 0))] * 2,
        out_shape=[tab, tab],
        compiler_params=pltpu.CompilerParams(dimension_semantics=("parallel",)),
        name="rope_table",
    )(pos, freq)


def _rope_rows(x1, x2, cos, sin):
    return x1 * cos - x2 * sin, x1 * sin + x2 * cos


def _in_proj_kernel(x_ref, cos_ref, sin_ref, g1_ref, win_ref, wint_ref, qagt_ref, kvagt_ref,
                    wuqt_ref, wukt_ref, wuvt_ref, gqt_ref, gknt_ref, gkrt_ref, qpadt_ref,
                    kpadt_ref, vpadt_ref,
                    qt_ref, k_ref, vt_ref, u_ref, gl_ref):
    tm = x_ref.shape[1]
    half = QK_ROPE_DIM // 2
    x = x_ref[0]
    h = x * lax.rsqrt(jnp.mean(x * x, axis=-1, keepdims=True) + EPS) * g1_ref[...]
    h = h.astype(_BF16)
    cos, sin = cos_ref[0], sin_ref[0]

    zt = lax.dot_general(wint_ref[...], h, _NT_DIMS, preferred_element_type=_F32)
    cqt = zt[_R_Q[0]:_R_Q[1]]
    cqt = cqt * lax.rsqrt(jnp.mean(cqt * cqt, axis=0, keepdims=True) + EPS) * _across(qagt_ref, tm)
    qt = jnp.dot(wuqt_ref[...], cqt.astype(_BF16), preferred_element_type=_F32)
    gqt = _across(gqt_ref, tm)
    qpadt = _across(qpadt_ref, tm)
    for hd in range(N_HEADS):
        qh = qt[hd * HEAD_PAD:(hd + 1) * HEAD_PAD]
        r = lax.rsqrt(jnp.sum(qh * qh, axis=0, keepdims=True) / QK_HEAD_DIM + EPS)
        qh = qh * r * gqt
        o1, o2 = _rope_rows(qh[QK_NOPE_DIM:QK_NOPE_DIM + half],
                            qh[QK_NOPE_DIM + half:QK_HEAD_DIM], cos, sin)
        qh = jnp.concatenate([qh[0:QK_NOPE_DIM], o1, o2, qh[QK_HEAD_DIM:]], axis=0) + qpadt
        qt_ref[0, hd * HEAD_PAD:(hd + 1) * HEAD_PAD, :] = qh.astype(_BF16)

    ckvt = zt[_R_KV[0]:_R_KV[1]]
    ckvt = (ckvt * lax.rsqrt(jnp.mean(ckvt * ckvt, axis=0, keepdims=True) + EPS)
            * _across(kvagt_ref, tm))
    ckvt = ckvt.astype(_BF16)
    vt = jnp.dot(wuvt_ref[...], ckvt, preferred_element_type=_F32)
    vt_ref[0] = (vt + _across(vpadt_ref, tm)).astype(_BF16)

    krt = zt[_R_KR[0]:_R_KR[1]]
    kr_ss = jnp.sum(krt * krt, axis=0, keepdims=True)
    krt = krt * _across(gkrt_ref, tm)
    kr1, kr2 = _rope_rows(krt[0:half], krt[half:QK_ROPE_DIM], cos, sin)
    knt = jnp.dot(wukt_ref[...], ckvt, preferred_element_type=_F32)
    gknt = _across(gknt_ref, tm)
    kpadt = _across(kpadt_ref, tm)
    for hd in range(N_HEADS):
        kn = knt[hd * QK_NOPE_DIM:(hd + 1) * QK_NOPE_DIM]
        ss = jnp.sum(kn * kn, axis=0, keepdims=True) + kr_ss
        r = lax.rsqrt(ss / QK_HEAD_DIM + EPS)
        kh = jnp.concatenate([kn * gknt * r, kr1 * r, kr2 * r, kpadt], axis=0)
        k_ref[0, :, hd * HEAD_PAD:(hd + 1) * HEAD_PAD] = kh.T.astype(_BF16)

    gl_ref[0] = jnp.dot(h, win_ref[:, _C_G[0]:_C_G[1]], preferred_element_type=_F32).astype(_BF16)
    u_ref[0] = jnp.dot(h, win_ref[:, _C_U[0]:_C_U[1]], preferred_element_type=_F32)


def _in_proj_call(x, tabs, consts, tm=512):
    b, s, _ = x.shape
    half = QK_ROPE_DIM // 2
    row = lambda w: pl.BlockSpec((1, tm, w), lambda bi, i: (bi, i, 0))
    col = lambda r: pl.BlockSpec((1, r, tm), lambda bi, i: (bi, 0, i))
    return pl.pallas_call(
        _in_proj_kernel,
        grid=(b, s // tm),
        in_specs=[row(D_MODEL), col(half), col(half)] + [_const_spec(c.shape) for c in consts],
        out_specs=[col(QK_PAD_DIM), row(QK_PAD_DIM), col(N_HEADS * V_ROWS), row(POOL_DIM),
                   row(2 * D_MODEL)],
        out_shape=[jax.ShapeDtypeStruct((b, QK_PAD_DIM, s), _BF16),
                   jax.ShapeDtypeStruct((b, s, QK_PAD_DIM), _BF16),
                   jax.ShapeDtypeStruct((b, N_HEADS * V_ROWS, s), _BF16),
                   jax.ShapeDtypeStruct((b, s, POOL_DIM), _F32),
                   jax.ShapeDtypeStruct((b, s, 2 * D_MODEL), _BF16)],
        compiler_params=pltpu.CompilerParams(dimension_semantics=("parallel", "parallel"),
                                             vmem_limit_bytes=VMEM_LIMIT),
        name="in_proj",
    )(x, *tabs, *consts)


def _finish_heads(accs, o_ref):
    ot = jnp.concatenate([a[0:V_HEAD_DIM] / a[V_HEAD_DIM:V_HEAD_DIM + 1] for a in accs], axis=0)
    o_ref[0] = ot.T.astype(o_ref.dtype)


def _attention_shifted_kernel(qt_ref, k_ref, vt_ref, o_ref, *, tk):
    n_kv = k_ref.shape[1] // tk
    chunks = [(hd, c) for hd in range(HEADS_PER_STEP) for c in range(n_kv)]
    qts = [qt_ref[0, hd * HEAD_PAD:(hd + 1) * HEAD_PAD, :] for hd in range(HEADS_PER_STEP)]

    def scores(hd, c):
        return jnp.dot(k_ref[0, c * tk:(c + 1) * tk, hd * HEAD_PAD:(hd + 1) * HEAD_PAD], qts[hd],
                       preferred_element_type=_F32)

    accs = [None] * HEADS_PER_STEP
    st_next = scores(*chunks[0])
    for i, (hd, c) in enumerate(chunks):
        st = st_next
        if i + 1 < len(chunks):
            st_next = scores(*chunks[i + 1])
        o = jnp.dot(vt_ref[0, hd * V_ROWS:(hd + 1) * V_ROWS, c * tk:(c + 1) * tk],
                    jnp.exp2(st).astype(_BF16), preferred_element_type=_F32)
        accs[hd] = o if accs[hd] is None else accs[hd] + o
    _finish_heads(accs, o_ref)


def _attention_online_kernel(qt_ref, k_ref, vt_ref, o_ref, *, tk):
    tq = qt_ref.shape[2]
    n_kv = k_ref.shape[1] // tk
    accs = []
    for hd in range(HEADS_PER_STEP):
        qt = qt_ref[0, hd * HEAD_PAD:(hd + 1) * HEAD_PAD, :]
        m = jnp.full((1, tq), -jnp.inf, _F32)
        acc = jnp.zeros((V_ROWS, tq), _F32)
        for c in range(n_kv):
            keys = slice(c * tk, (c + 1) * tk)
            st = jnp.dot(k_ref[0, keys, hd * HEAD_PAD:(hd + 1) * HEAD_PAD], qt,
                         preferred_element_type=_F32)
            m_new = jnp.maximum(m, jnp.max(st, axis=0, keepdims=True))
            pt = jnp.exp2(st - m_new).astype(_BF16)
            acc = jnp.exp2(m - m_new) * acc + jnp.dot(
                vt_ref[0, hd * V_ROWS:(hd + 1) * V_ROWS, keys], pt, preferred_element_type=_F32)
            m = m_new
        accs.append(acc)
    _finish_heads(accs, o_ref)


def _attention_call(body, qt, k, vt, tq, tk, name):
    b, s, _ = k.shape
    qw = HEADS_PER_STEP * HEAD_PAD
    vw = HEADS_PER_STEP * V_ROWS
    ow = HEADS_PER_STEP * V_HEAD_DIM
    return pl.pallas_call(
        functools.partial(body, tk=tk),
        grid=(b, N_HEADS // HEADS_PER_STEP, s // tq),
        in_specs=[pl.BlockSpec((1, qw, tq), lambda bi, hp, qi: (bi, hp, qi)),
                  pl.BlockSpec((1, s, qw), lambda bi, hp, qi: (bi, 0, hp)),
                  pl.BlockSpec((1, vw, s), lambda bi, hp, qi: (bi, hp, 0))],
        out_specs=pl.BlockSpec((1, tq, ow), lambda bi, hp, qi: (bi, qi, hp)),
        out_shape=jax.ShapeDtypeStruct((b, s, ATTN_OUT_DIM), _BF16),
        compiler_params=pltpu.CompilerParams(
            dimension_semantics=("parallel", "parallel", "arbitrary"),
            vmem_limit_bytes=VMEM_LIMIT),
        name=name,
    )(qt, k, vt)


def _attention(qt, k, vt, use_shift):
    shifted = functools.partial(_attention_call, _attention_shifted_kernel,
                                tq=1024, tk=256, name="attention_shifted")
    online = functools.partial(_attention_call, _attention_online_kernel,
                               tq=512, tk=512, name="attention_online")
    return lax.cond(use_shift, shifted, online, qt, k, vt)


def _route_rows(logit_rows, bias_ref):
    scores = [_sigmoid(r) for r in logit_rows]
    biased = [scores[e] + bias_ref[e] for e in range(N_EXPERTS)]
    gscore = []
    for g in range(N_EXPERT_GROUPS):
        a, b, c, d = biased[g * EXPERTS_PER_GROUP:(g + 1) * EXPERTS_PER_GROUP]
        hi1, lo1 = jnp.maximum(a, b), jnp.minimum(a, b)
        hi2, lo2 = jnp.maximum(c, d), jnp.minimum(c, d)
        top = jnp.maximum(hi1, hi2)
        second = jnp.maximum(jnp.minimum(hi1, hi2), jnp.maximum(lo1, lo2))
        gscore.append(top + second)
    best = gscore[0]
    bg = jnp.zeros_like(best, dtype=jnp.int32)
    for g in range(1, N_EXPERT_GROUPS):
        better = gscore[g] > best
        best = jnp.where(better, gscore[g], best)
        bg = jnp.where(better, g, bg)
    vb, vs = [], []
    for i in range(EXPERTS_PER_GROUP):
        b_i, s_i = biased[i], scores[i]
        for g in range(1, N_EXPERT_GROUPS):
            pick = bg == g
            b_i = jnp.where(pick, biased[g * EXPERTS_PER_GROUP + i], b_i)
            s_i = jnp.where(pick, scores[g * EXPERTS_PER_GROUP + i], s_i)
        vb.append(b_i)
        vs.append(s_i)
    i1 = jnp.zeros_like(bg)
    b1 = vb[0]
    for i in range(1, EXPERTS_PER_GROUP):
        better = vb[i] > b1
        b1 = jnp.where(better, vb[i], b1)
        i1 = jnp.where(better, i, i1)
    i2 = jnp.full_like(bg, -1)
    b2 = jnp.full_like(b1, -jnp.inf)
    for i in range(EXPERTS_PER_GROUP):
        better = (i1 != i) & ((vb[i] > b2) | (i2 < 0))
        b2 = jnp.where(better, vb[i], b2)
        i2 = jnp.where(better, i, i2)
    s1 = vs[0]
    s2 = vs[0]
    for i in range(1, EXPERTS_PER_GROUP):
        s1 = jnp.where(i1 == i, vs[i], s1)
        s2 = jnp.where(i2 == i, vs[i], s2)
    denom = s1 + s2
    gates = []
    for e in range(N_EXPERTS):
        g, i = divmod(e, EXPERTS_PER_GROUP)
        chosen = (bg == g) & ((i1 == i) | (i2 == i))
        gates.append(jnp.where(chosen, scores[e] / denom, 0.0))
    return gates


def _window_sums(e, w, tm):
    n = e.shape[0]
    span = 1
    while span < w:
        e = e + pltpu.roll(e, n - span, 0)
        span *= 2
    first = POOL_HALO - w // 2
    if first:
        e = pltpu.roll(e, n - first, 0)
    return e[0:tm]


def _merge_kernel(bias_ref, a_ref, u_ref, up_ref, un_ref, gl_ref, x_ref, edge_ref,
                  wpool_ref, pscale_ref, wa_ref, wp_ref, wo_ref, g2_ref, wrh_ref, wrl_ref,
                  x1_ref, h2_ref, gates_ref, ext_ref, gt_ref, *, n_i):
    tm = u_ref.shape[0]
    i = pl.program_id(0) % n_i

    first_tile = i == 0
    last_tile = i == n_i - 1
    ext_ref[0:POOL_HALO, :] = jnp.where(first_tile, 0.0, up_ref[...])
    ext_ref[POOL_HALO:POOL_HALO + tm, :] = u_ref[...]
    ext_ref[POOL_HALO + tm:, :] = jnp.where(last_tile, 0.0, un_ref[...])
    mixed = []
    for g, w in enumerate(POOL_WINDOWS):
        cols = slice(g * POOL_GROUP_DIM, (g + 1) * POOL_GROUP_DIM)
        sums = _window_sums(ext_ref[:, cols], w, tm)
        top = sums[0:POOL_HALO] * jnp.where(first_tile, edge_ref[0, :, cols], 1.0 / w)
        bot = sums[tm - POOL_HALO:] * jnp.where(last_tile, edge_ref[1, :, cols], 1.0 / w)
        mean = jnp.concatenate([top, sums[POOL_HALO:tm - POOL_HALO] * (1.0 / w), bot], axis=0)
        pooled = mean - u_ref[:, cols]
        mixed.append(jnp.dot(pooled.astype(_BF16), wpool_ref[g], preferred_element_type=_F32))
    mixed = jnp.concatenate(mixed, axis=-1) * pscale_ref[...]

    a = jnp.dot(a_ref[...], wa_ref[...], preferred_element_type=_F32)
    p = jnp.dot(mixed.astype(_BF16), wp_ref[...], preferred_element_type=_F32)
    ga = _sigmoid(gl_ref[:, 0:D_MODEL].astype(_F32))
    gp = _sigmoid(gl_ref[:, D_MODEL:2 * D_MODEL].astype(_F32))
    m = ga * a + gp * p
    x1 = x_ref[...] + jnp.dot(m.astype(_BF16), wo_ref[...], preferred_element_type=_F32)
    x1_ref[...] = x1

    h2 = x1 * lax.rsqrt(jnp.mean(x1 * x1, axis=-1, keepdims=True) + EPS) * g2_ref[...]
    h2_hi = h2.astype(_BF16)
    h2_ref[...] = h2_hi

    h2_lo = (h2 - h2_hi.astype(_F32)).astype(_BF16)
    logits = (jnp.dot(h2_hi, wrh_ref[...], preferred_element_type=_F32)
              + jnp.dot(h2_lo, wrh_ref[...], preferred_element_type=_F32)
              + jnp.dot(h2_hi, wrl_ref[...], preferred_element_type=_F32))
    lt = logits.T
    gate_rows = _route_rows([lt[e:e + 1, :] for e in range(N_EXPERTS)], bias_ref)
    gt_ref[...] = jnp.zeros_like(gt_ref)
    for e in range(N_EXPERTS):
        gt_ref[e:e + 1, :] = gate_rows[e]

    gates_ref[...] = gt_ref[...].T[:, 0:N_EXPERTS]


def _merge_call(bias, a, u, gl, x, edge, merge_w, seq_len, tm=256):
    n = x.shape[0]
    hb = tm // POOL_HALO
    n_hb = n // POOL_HALO
    tile = lambda w: pl.BlockSpec((tm, w), lambda s: (s, 0))
    return pl.pallas_call(
        functools.partial(_merge_kernel, n_i=seq_len // tm),
        grid=(n // tm,),
        in_specs=[pl.BlockSpec(memory_space=pltpu.SMEM),
                  tile(ATTN_OUT_DIM), tile(POOL_DIM),
                  pl.BlockSpec((POOL_HALO, POOL_DIM), lambda s: (jnp.maximum(s * hb - 1, 0), 0)),
                  pl.BlockSpec((POOL_HALO, POOL_DIM),
                               lambda s: (jnp.minimum((s + 1) * hb, n_hb - 1), 0)),
                  tile(2 * D_MODEL), tile(D_MODEL), _const_spec(edge.shape)]
                 + [_const_spec(w.shape) for w in merge_w],
        out_specs=[tile(D_MODEL), tile(D_MODEL), tile(N_EXPERTS)],
        out_shape=[jax.ShapeDtypeStruct((n, D_MODEL), _F32),
                   jax.ShapeDtypeStruct((n, D_MODEL), _BF16),
                   jax.ShapeDtypeStruct((n, N_EXPERTS), _F32)],
        scratch_shapes=[pltpu.VMEM((tm + 2 * POOL_HALO, POOL_DIM), _F32),
                        pltpu.VMEM((LANES, tm), _F32)],
        compiler_params=pltpu.CompilerParams(dimension_semantics=("parallel",),
                                             vmem_limit_bytes=VMEM_LIMIT),
        name="merge_route",
    )(bias, a, u, u, u, gl, x, edge, *merge_w)


def _moe_kernel(h_ref, gates_ref, x_ref, wg_ref, wu_ref, wd_ref, o_ref):
    h = h_ref[...]
    y = x_ref[...]
    for e in range(N_EXPERTS):
        gate = jnp.dot(h, wg_ref[e], preferred_element_type=_F32)
        up = jnp.dot(h, wu_ref[e], preferred_element_type=_F32)
        act = gate * _sigmoid(gate) * up
        y = y + gates_ref[:, e:e + 1] * jnp.dot(act.astype(_BF16), wd_ref[e],
                                                preferred_element_type=_F32)
    o_ref[...] = y


def _moe_call(h2, gates, x1, wg, wu, wd, tm=512):
    n = h2.shape[0]
    row = lambda w: pl.BlockSpec((tm, w), lambda i: (i, 0))
    return pl.pallas_call(
        _moe_kernel,
        grid=(n // tm,),
        in_specs=[row(D_MODEL), row(N_EXPERTS), row(D_MODEL),
                  _const_spec(wg.shape), _const_spec(wu.shape), _const_spec(wd.shape)],
        out_specs=row(D_MODEL),
        out_shape=jax.ShapeDtypeStruct((n, D_MODEL), _F32),
        compiler_params=pltpu.CompilerParams(dimension_semantics=("parallel",),
                                             vmem_limit_bytes=VMEM_LIMIT),
        name="moe",
    )(h2, gates, x1, wg, wu, wd)


def _pad_heads(w, head_dim):
    lead = w.shape[:-1]
    w = w.reshape(*lead, N_HEADS, head_dim)
    w = jnp.pad(w, [(0, 0)] * len(lead) + [(0, 0), (0, HEAD_PAD - head_dim)])
    return w.reshape(*lead, N_HEADS * HEAD_PAD)


def _rows(v):
    return jnp.broadcast_to(v.astype(_F32)[:, None], (v.shape[0], LANES))


def _prep_layer(l, w_in, w_uq, w_ukv, q_a_g, kv_a_g, q_norm_g, k_norm_g, norm1_g, kshift):
    q_end = Q_LORA_RANK
    kv_end = q_end + KV_LORA_RANK
    kr_end = kv_end + QK_ROPE_DIM
    wi = w_in[l]
    win = wi[:, kr_end:].astype(_BF16)
    wint = wi[:, :kr_end].T.astype(_BF16)
    wuqt = _pad_heads(w_uq[l], QK_HEAD_DIM).T.astype(_BF16)
    wkv = w_ukv[l].reshape(KV_LORA_RANK, N_HEADS, QK_NOPE_DIM + V_HEAD_DIM)
    wukt = wkv[:, :, :QK_NOPE_DIM].reshape(KV_LORA_RANK, -1).T.astype(_BF16)
    wuvt = jnp.pad(wkv[:, :, QK_NOPE_DIM:], ((0, 0), (0, 0), (0, V_ROWS - V_HEAD_DIM)))
    wuvt = wuvt.reshape(KV_LORA_RANK, N_HEADS * V_ROWS).T.astype(_BF16)

    pad = HEAD_PAD - QK_HEAD_DIM
    gq = jnp.pad(q_norm_g[l], (0, pad))
    gk = k_norm_g[l]
    shift_pos = (jnp.arange(HEAD_PAD) == SHIFT_LANE).astype(_F32)
    ones_row = (jnp.arange(V_ROWS) == V_HEAD_DIM).astype(_F32)
    return (norm1_g[l].reshape(1, -1), win, wint, _rows(q_a_g[l]), _rows(kv_a_g[l]),
            wuqt, wukt, wuvt,
            _rows(gq * (ATTN_SCALE * LOG2E)), _rows(gk[:QK_NOPE_DIM]), _rows(gk[QK_NOPE_DIM:]),
            _rows(shift_pos), _rows(shift_pos[QK_HEAD_DIM:] * kshift),
            _rows(jnp.tile(ones_row, N_HEADS)))


def _pool_edge_scales(seq_len):
    t = jnp.concatenate([jnp.arange(POOL_HALO), jnp.arange(seq_len - POOL_HALO, seq_len)])
    per_group = []
    for w in POOL_WINDOWS:
        left = w // 2
        right = w - 1 - left
        cnt = jnp.minimum(t + right + 1, seq_len) - jnp.maximum(t - left, 0)
        per_group.append(jnp.broadcast_to((1.0 / cnt.astype(_F32))[:, None],
                                          (2 * POOL_HALO, POOL_GROUP_DIM)))
    return jnp.concatenate(per_group, axis=1).reshape(2, POOL_HALO, POOL_DIM)


def kernel(x, positions, norm1_g, w_in, q_a_g, kv_a_g, w_uq, w_ukv, q_norm_g, k_norm_g,
           w_pool, pool_scale, w_branch_attn, w_branch_pool, w_out, norm2_g,
           w_router, router_bias, w_expert_gate, w_expert_up, w_expert_down):
    b, s, d = x.shape
    n = b * s
    depth = w_in.shape[0]

    inv_freq = 1.0 / (ROPE_THETA ** (jnp.arange(0, QK_ROPE_DIM, 2, dtype=_F32) / QK_ROPE_DIM))
    tabs = _rope_table_call(positions.astype(_F32).reshape(b, 1, s), _rows(inv_freq))

    wr = jnp.pad(w_router.astype(_F32), ((0, 0), (0, LANES - N_EXPERTS)))
    wrh = wr.astype(_BF16)
    wrl = (wr - wrh.astype(_F32)).astype(_BF16)
    bias = router_bias.astype(_F32)
    edge = _pool_edge_scales(s)

    for l in range(depth):
        bound = ((1.02 * QK_HEAD_DIM ** 0.5) * jnp.max(jnp.abs(q_norm_g[l]))
                 * jnp.max(jnp.abs(k_norm_g[l])))
        use_shift = bound <= SCORE_BOUND_LIMIT
        kshift = jnp.where(use_shift, -bound * LOG2E, 0.0)
        consts = _prep_layer(l, w_in, w_uq, w_ukv, q_a_g, kv_a_g, q_norm_g, k_norm_g, norm1_g,
                             kshift)
        qt, k, vt, u, gl = _in_proj_call(x, tabs, consts)
        a = _attention(qt, k, vt, use_shift)
        merge_w = [w_pool[l].astype(_BF16), pool_scale[l].reshape(1, -1),
                   w_branch_attn[l].astype(_BF16), w_branch_pool[l].astype(_BF16),
                   w_out[l].astype(_BF16), norm2_g[l].reshape(1, d), wrh, wrl]
        x1, h2, gates = _merge_call(bias, a.reshape(n, -1), u.reshape(n, -1), gl.reshape(n, -1),
                                    x.reshape(n, d), edge, merge_w, s)
        x = _moe_call(h2, gates, x1, w_expert_gate[l].astype(_BF16),
                      w_expert_up[l].astype(_BF16),
                      w_expert_down[l].astype(_BF16)).reshape(b, s, d)
    return x
```

```python
import functools

import jax
import jax.numpy as jnp
from jax import lax
from jax.experimental import pallas as pl
from jax.experimental.pallas import tpu as pltpu

D_MODEL = 1024
N_HEADS = 8
QK_NOPE_DIM = 64
QK_ROPE_DIM = 32
QK_HEAD_DIM = QK_NOPE_DIM + QK_ROPE_DIM
V_HEAD_DIM = 64
Q_LORA_RANK = 384
KV_LORA_RANK = 256
ROPE_THETA = 10000.0
ATTN_SCALE = QK_HEAD_DIM ** -0.5
POOL_WINDOWS = (2, 4, 8, 16)
N_POOL_GROUPS = 4
POOL_DIM = D_MODEL // 2
POOL_GROUP_DIM = POOL_DIM // N_POOL_GROUPS
ATTN_OUT_DIM = N_HEADS * V_HEAD_DIM
N_EXPERTS = 16
N_EXPERT_GROUPS = 4
EXPERTS_PER_GROUP = N_EXPERTS // N_EXPERT_GROUPS
D_EXPERT = 256
EPS = 1e-6

LANES = 128
SUBLANES = 8
HEAD_PAD = LANES
QK_PAD_DIM = N_HEADS * HEAD_PAD
HEADS_PER_STEP = 2
POOL_HALO = 8
SHIFT_LANE = QK_HEAD_DIM
BF16_ROWS = 16
V_ROWS = 80
LOG2E = 1.4426950408889634
ROUTE_HI, ROUTE_LO, ROUTE_GROUP = 0, EXPERTS_PER_GROUP, 2 * EXPERTS_PER_GROUP
MOE_TILE = 512
MOE_CHUNK = 160
MOE_CHUNK_PAD = 256
SCORE_BOUND_LIMIT = 32.0

_C_U = (0, POOL_DIM)
_C_G = (_C_U[1], _C_U[1] + 2 * D_MODEL)
_R_Q = (0, Q_LORA_RANK)
_R_KV = (_R_Q[1], _R_Q[1] + KV_LORA_RANK)
_R_KR = (_R_KV[1], _R_KV[1] + QK_ROPE_DIM)

VMEM_LIMIT = 56 * 1024 * 1024

_F32 = jnp.float32
_BF16 = jnp.bfloat16
_NT_DIMS = (((1,), (1,)), ((), ()))


def _const_spec(shape):
    nd = len(shape)
    return pl.BlockSpec(shape, lambda *_: (0,) * nd, pipeline_mode=pl.Buffered(1))


def _sigmoid(x):
    return 1.0 / (1.0 + jnp.exp2(x * (-LOG2E)))


def _across(col_ref, n):
    return jnp.tile(col_ref[...], (1, n // LANES))


def _rope_table_kernel(pos_ref, freq_ref, cos_ref, sin_ref):
    ang = _across(freq_ref, pos_ref.shape[2]) * pos_ref[0]
    cos_ref[0] = jnp.cos(ang)
    sin_ref[0] = jnp.sin(ang)


def _rope_table_call(pos, freq):
    b, _, s = pos.shape
    half = QK_ROPE_DIM // 2
    tab = jax.ShapeDtypeStruct((b, half, s), _F32)
    return pl.pallas_call(
        _rope_table_kernel,
        grid=(b,),
        in_specs=[pl.BlockSpec((1, 1, s), lambda i: (i, 0, 0)), _const_spec(freq.shape)],
        out_specs=[pl.BlockSpec((1, half, s), lambda i: (i, 0, 0))] * 2,
        out_shape=[tab, tab],
        compiler_params=pltpu.CompilerParams(dimension_semantics=("parallel",)),
        name="rope_table",
    )(pos, freq)


def _rope_rows(x1, x2, cos, sin):
    return x1 * cos - x2 * sin, x1 * sin + x2 * cos


def _in_proj_kernel(x_ref, cos_ref, sin_ref, g1_ref, win_ref, wint_ref, qagt_ref, kvagt_ref,
                    wuqt_ref, wukt_ref, wuvt_ref, gqt_ref, gknt_ref, gkrt_ref, qpadt_ref,
                    kpadt_ref, vpadt_ref,
                    qt_ref, k_ref, vt_ref, u_ref, gl_ref):
    tm = x_ref.shape[1]
    half = QK_ROPE_DIM // 2
    x = x_ref[0]
    h = x * lax.rsqrt(jnp.mean(x * x, axis=-1, keepdims=True) + EPS) * g1_ref[...]
    h = h.astype(_BF16)
    cos, sin = cos_ref[0], sin_ref[0]

    zt = lax.dot_general(wint_ref[...], h, _NT_DIMS, preferred_element_type=_F32)
    cqt = zt[_R_Q[0]:_R_Q[1]]
    cqt = cqt * lax.rsqrt(jnp.mean(cqt * cqt, axis=0, keepdims=True) + EPS) * _across(qagt_ref, tm)
    qt = jnp.dot(wuqt_ref[...], cqt.astype(_BF16), preferred_element_type=_F32)
    gqt = _across(gqt_ref, tm)
    qpadt = _across(qpadt_ref, tm)
    for hd in range(N_HEADS):
        qh = qt[hd * HEAD_PAD:(hd + 1) * HEAD_PAD]
        r = lax.rsqrt(jnp.sum(qh * qh, axis=0, keepdims=True) / QK_HEAD_DIM + EPS)
        qh = qh * r * gqt
        o1, o2 = _rope_rows(qh[QK_NOPE_DIM:QK_NOPE_DIM + half],
                            qh[QK_NOPE_DIM + half:QK_HEAD_DIM], cos, sin)
        qh = jnp.concatenate([qh[0:QK_NOPE_DIM], o1, o2, qh[QK_HEAD_DIM:]], axis=0) + qpadt
        qt_ref[0, hd * HEAD_PAD:(hd + 1) * HEAD_PAD, :] = qh.astype(_BF16)

    ckvt = zt[_R_KV[0]:_R_KV[1]]
    ckvt = (ckvt * lax.rsqrt(jnp.mean(ckvt * ckvt, axis=0, keepdims=True) + EPS)
            * _across(kvagt_ref, tm))
    ckvt = ckvt.astype(_BF16)
    vt = jnp.dot(wuvt_ref[...], ckvt, preferred_element_type=_F32)
    vt_ref[0] = (vt + _across(vpadt_ref, tm)).astype(_BF16)

    krt = zt[_R_KR[0]:_R_KR[1]]
    kr_ss = jnp.sum(krt * krt, axis=0, keepdims=True)
    krt = krt * _across(gkrt_ref, tm)
    kr1, kr2 = _rope_rows(krt[0:half], krt[half:QK_ROPE_DIM], cos, sin)
    knt = jnp.dot(wukt_ref[...], ckvt, preferred_element_type=_F32)
    gknt = _across(gknt_ref, tm)
    kpadt = _across(kpadt_ref, tm)
    for hd in range(N_HEADS):
        kn = knt[hd * QK_NOPE_DIM:(hd + 1) * QK_NOPE_DIM]
        ss = jnp.sum(kn * kn, axis=0, keepdims=True) + kr_ss
        r = lax.rsqrt(ss / QK_HEAD_DIM + EPS)
        kh = jnp.concatenate([kn * gknt * r, kr1 * r, kr2 * r, kpadt], axis=0)
        k_ref[0, :, hd * HEAD_PAD:(hd + 1) * HEAD_PAD] = kh.T.astype(_BF16)

    gl_ref[0] = jnp.dot(h, win_ref[:, _C_G[0]:_C_G[1]], preferred_element_type=_F32).astype(_BF16)
    u_ref[0] = jnp.dot(h, win_ref[:, _C_U[0]:_C_U[1]], preferred_element_type=_F32)


def _in_proj_call(x, tabs, consts, tm=512):
    b, s, _ = x.shape
    half = QK_ROPE_DIM // 2
    row = lambda w: pl.BlockSpec((1, tm, w), lambda bi, i: (bi, i, 0))
    col = lambda r: pl.BlockSpec((1, r, tm), lambda bi, i: (bi, 0, i))
    return pl.pallas_call(
        _in_proj_kernel,
        grid=(b, s // tm),
        in_specs=[row(D_MODEL), col(half), col(half)] + [_const_spec(c.shape) for c in consts],
        out_specs=[col(QK_PAD_DIM), row(QK_PAD_DIM), col(N_HEADS * V_ROWS), row(POOL_DIM),
                   row(2 * D_MODEL)],
        out_shape=[jax.ShapeDtypeStruct((b, QK_PAD_DIM, s), _BF16),
                   jax.ShapeDtypeStruct((b, s, QK_PAD_DIM), _BF16),
                   jax.ShapeDtypeStruct((b, N_HEADS * V_ROWS, s), _BF16),
                   jax.ShapeDtypeStruct((b, s, POOL_DIM), _F32),
                   jax.ShapeDtypeStruct((b, s, 2 * D_MODEL), _BF16)],
        compiler_params=pltpu.CompilerParams(dimension_semantics=("parallel", "parallel"),
                                             vmem_limit_bytes=VMEM_LIMIT),
        name="in_proj",
    )(x, *tabs, *consts)


def _finish_heads(accs, o_ref):
    ot = jnp.concatenate([a[0:V_HEAD_DIM] / a[V_HEAD_DIM:V_HEAD_DIM + 1] for a in accs], axis=0)
    o_ref[0] = ot.T.astype(o_ref.dtype)


def _attention_shifted_kernel(qt_ref, k_ref, vt_ref, o_ref, *, tk):
    n_kv = k_ref.shape[1] // tk
    chunks = [(hd, c) for hd in range(HEADS_PER_STEP) for c in range(n_kv)]
    qts = [qt_ref[0, hd * HEAD_PAD:(hd + 1) * HEAD_PAD, :] for hd in range(HEADS_PER_STEP)]

    def scores(hd, c):
        return jnp.dot(k_ref[0, c * tk:(c + 1) * tk, hd * HEAD_PAD:(hd + 1) * HEAD_PAD], qts[hd],
                       preferred_element_type=_F32)

    accs = [None] * HEADS_PER_STEP
    st_next = scores(*chunks[0])
    for i, (hd, c) in enumerate(chunks):
        st = st_next
        if i + 1 < len(chunks):
            st_next = scores(*chunks[i + 1])
        o = jnp.dot(vt_ref[0, hd * V_ROWS:(hd + 1) * V_ROWS, c * tk:(c + 1) * tk],
                    jnp.exp2(st).astype(_BF16), preferred_element_type=_F32)
        accs[hd] = o if accs[hd] is None else accs[hd] + o
    _finish_heads(accs, o_ref)


def _attention_online_kernel(qt_ref, k_ref, vt_ref, o_ref, *, tk):
    tq = qt_ref.shape[2]
    n_kv = k_ref.shape[1] // tk
    accs = []
    for hd in range(HEADS_PER_STEP):
        qt = qt_ref[0, hd * HEAD_PAD:(hd + 1) * HEAD_PAD, :]
        m = jnp.full((1, tq), -jnp.inf, _F32)
        acc = jnp.zeros((V_ROWS, tq), _F32)
        for c in range(n_kv):
            keys = slice(c * tk, (c + 1) * tk)
            st = jnp.dot(k_ref[0, keys, hd * HEAD_PAD:(hd + 1) * HEAD_PAD], qt,
                         preferred_element_type=_F32)
            m_new = jnp.maximum(m, jnp.max(st, axis=0, keepdims=True))
            pt = jnp.exp2(st - m_new).astype(_BF16)
            acc = jnp.exp2(m - m_new) * acc + jnp.dot(
                vt_ref[0, hd * V_ROWS:(hd + 1) * V_ROWS, keys], pt, preferred_element_type=_F32)
            m = m_new
        accs.append(acc)
    _finish_heads(accs, o_ref)


def _attention_call(body, qt, k, vt, tq, tk, name):
    b, s, _ = k.shape
    qw = HEADS_PER_STEP * HEAD_PAD
    vw = HEADS_PER_STEP * V_ROWS
    ow = HEADS_PER_STEP * V_HEAD_DIM
    return pl.pallas_call(
        functools.partial(body, tk=tk),
        grid=(b, N_HEADS // HEADS_PER_STEP, s // tq),
        in_specs=[pl.BlockSpec((1, qw, tq), lambda bi, hp, qi: (bi, hp, qi)),
                  pl.BlockSpec((1, s, qw), lambda bi, hp, qi: (bi, 0, hp)),
                  pl.BlockSpec((1, vw, s), lambda bi, hp, qi: (bi, hp, 0))],
        out_specs=pl.BlockSpec((1, tq, ow), lambda bi, hp, qi: (bi, qi, hp)),
        out_shape=jax.ShapeDtypeStruct((b, s, ATTN_OUT_DIM), _BF16),
        compiler_params=pltpu.CompilerParams(
            dimension_semantics=("parallel", "parallel", "arbitrary"),
            vmem_limit_bytes=VMEM_LIMIT),
        name=name,
    )(qt, k, vt)


def _attention(qt, k, vt, use_shift):
    shifted = functools.partial(_attention_call, _attention_shifted_kernel,
                                tq=1024, tk=256, name="attention_shifted")
    online = functools.partial(_attention_call, _attention_online_kernel,
                               tq=512, tk=512, name="attention_online")
    return lax.cond(use_shift, shifted, online, qt, k, vt)


def _route_rows(logit_rows, bias_ref):
    scores = [_sigmoid(r) for r in logit_rows]
    biased = [scores[e] + bias_ref[e] for e in range(N_EXPERTS)]
    gscore = []
    for g in range(N_EXPERT_GROUPS):
        a, b, c, d = biased[g * EXPERTS_PER_GROUP:(g + 1) * EXPERTS_PER_GROUP]
        hi1, lo1 = jnp.maximum(a, b), jnp.minimum(a, b)
        hi2, lo2 = jnp.maximum(c, d), jnp.minimum(c, d)
        top = jnp.maximum(hi1, hi2)
        second = jnp.maximum(jnp.minimum(hi1, hi2), jnp.maximum(lo1, lo2))
        gscore.append(top + second)
    best = gscore[0]
    bg = jnp.zeros_like(best, dtype=jnp.int32)
    for g in range(1, N_EXPERT_GROUPS):
        better = gscore[g] > best
        best = jnp.where(better, gscore[g], best)
        bg = jnp.where(better, g, bg)
    vb, vs = [], []
    for i in range(EXPERTS_PER_GROUP):
        b_i, s_i = biased[i], scores[i]
        for g in range(1, N_EXPERT_GROUPS):
            pick = bg == g
            b_i = jnp.where(pick, biased[g * EXPERTS_PER_GROUP + i], b_i)
            s_i = jnp.where(pick, scores[g * EXPERTS_PER_GROUP + i], s_i)
        vb.append(b_i)
        vs.append(s_i)
    i1 = jnp.zeros_like(bg)
    b1 = vb[0]
    for i in range(1, EXPERTS_PER_GROUP):
        better = vb[i] > b1
        b1 = jnp.where(better, vb[i], b1)
        i1 = jnp.where(better, i, i1)
    i2 = jnp.full_like(bg, -1)
    b2 = jnp.full_like(b1, -jnp.inf)
    for i in range(EXPERTS_PER_GROUP):
        better = (i1 != i) & ((vb[i] > b2) | (i2 < 0))
        b2 = jnp.where(better, vb[i], b2)
        i2 = jnp.where(better, i, i2)
    s1 = vs[0]
    s2 = vs[0]
    for i in range(1, EXPERTS_PER_GROUP):
        s1 = jnp.where(i1 == i, vs[i], s1)
        s2 = jnp.where(i2 == i, vs[i], s2)
    denom = s1 + s2
    in_group = [jnp.where((i1 == i) | (i2 == i), vs[i] / denom, 0.0)
                for i in range(EXPERTS_PER_GROUP)]
    group_onehot = [jnp.where(bg == g, 1.0, 0.0) for g in range(N_EXPERT_GROUPS)]
    return in_group, group_onehot


def _window_sums(e, w, tm):
    n = e.shape[0]
    span = 1
    while span < w:
        e = e + pltpu.roll(e, n - span, 0)
        span *= 2
    first = POOL_HALO - w // 2
    if first:
        e = pltpu.roll(e, n - first, 0)
    return e[0:tm]


def _merge_kernel(bias_ref, a_ref, u_ref, up_ref, un_ref, gl_ref, x_ref, edge_ref,
                  wpool_ref, pscale_ref, wa_ref, wp_ref, wo_ref, g2_ref, wrh_ref, wrl_ref,
                  x1_ref, h2_ref, route_ref, ext_ref, gt_ref, *, n_i):
    tm = u_ref.shape[0]
    i = pl.program_id(0) % n_i

    first_tile = i == 0
    last_tile = i == n_i - 1
    ext_ref[0:POOL_HALO, :] = jnp.where(first_tile, 0.0, up_ref[...])
    ext_ref[POOL_HALO:POOL_HALO + tm, :] = u_ref[...]
    ext_ref[POOL_HALO + tm:, :] = jnp.where(last_tile, 0.0, un_ref[...])
    mixed = []
    for g, w in enumerate(POOL_WINDOWS):
        cols = slice(g * POOL_GROUP_DIM, (g + 1) * POOL_GROUP_DIM)
        sums = _window_sums(ext_ref[:, cols], w, tm)
        top = sums[0:POOL_HALO] * jnp.where(first_tile, edge_ref[0, :, cols], 1.0 / w)
        bot = sums[tm - POOL_HALO:] * jnp.where(last_tile, edge_ref[1, :, cols], 1.0 / w)
        mean = jnp.concatenate([top, sums[POOL_HALO:tm - POOL_HALO] * (1.0 / w), bot], axis=0)
        pooled = mean - u_ref[:, cols]
        mixed.append(jnp.dot(pooled.astype(_BF16), wpool_ref[g], preferred_element_type=_F32))
    mixed = jnp.concatenate(mixed, axis=-1) * pscale_ref[...]

    a = jnp.dot(a_ref[...], wa_ref[...], preferred_element_type=_F32)
    p = jnp.dot(mixed.astype(_BF16), wp_ref[...], preferred_element_type=_F32)
    ga = _sigmoid(gl_ref[:, 0:D_MODEL].astype(_F32))
    gp = _sigmoid(gl_ref[:, D_MODEL:2 * D_MODEL].astype(_F32))
    m = ga * a + gp * p
    x1 = x_ref[...] + jnp.dot(m.astype(_BF16), wo_ref[...], preferred_element_type=_F32)
    x1_ref[...] = x1

    h2 = x1 * lax.rsqrt(jnp.mean(x1 * x1, axis=-1, keepdims=True) + EPS) * g2_ref[...]
    h2_hi = h2.astype(_BF16)
    h2_ref[...] = h2_hi

    h2_lo = (h2 - h2_hi.astype(_F32)).astype(_BF16)
    logits = (jnp.dot(h2_hi, wrh_ref[...], preferred_element_type=_F32)
              + jnp.dot(h2_lo, wrh_ref[...], preferred_element_type=_F32)
              + jnp.dot(h2_hi, wrl_ref[...], preferred_element_type=_F32))
    lt = logits.T
    in_group, group_onehot = _route_rows([lt[e:e + 1, :] for e in range(N_EXPERTS)], bias_ref)
    gt_ref[...] = jnp.zeros_like(gt_ref)
    for j, w in enumerate(in_group):
        hi = w.astype(_BF16).astype(_F32)
        gt_ref[ROUTE_HI + j:ROUTE_HI + j + 1, :] = hi
        gt_ref[ROUTE_LO + j:ROUTE_LO + j + 1, :] = w - hi
    for g, onehot in enumerate(group_onehot):
        gt_ref[ROUTE_GROUP + g:ROUTE_GROUP + g + 1, :] = onehot
    route_ref[...] = gt_ref[...].T.astype(_BF16)


def _merge_call(bias, a, u, gl, x, edge, merge_w, seq_len, tm=256):
    n = x.shape[0]
    hb = tm // POOL_HALO
    n_hb = n // POOL_HALO
    tile = lambda w: pl.BlockSpec((tm, w), lambda s: (s, 0))
    return pl.pallas_call(
        functools.partial(_merge_kernel, n_i=seq_len // tm),
        grid=(n // tm,),
        in_specs=[pl.BlockSpec(memory_space=pltpu.SMEM),
                  tile(ATTN_OUT_DIM), tile(POOL_DIM),
                  pl.BlockSpec((POOL_HALO, POOL_DIM), lambda s: (jnp.maximum(s * hb - 1, 0), 0)),
                  pl.BlockSpec((POOL_HALO, POOL_DIM),
                               lambda s: (jnp.minimum((s + 1) * hb, n_hb - 1), 0)),
                  tile(2 * D_MODEL), tile(D_MODEL), _const_spec(edge.shape)]
                 + [_const_spec(w.shape) for w in merge_w],
        out_specs=[tile(D_MODEL), tile(D_MODEL), tile(LANES)],
        out_shape=[jax.ShapeDtypeStruct((n, D_MODEL), _F32),
                   jax.ShapeDtypeStruct((n, D_MODEL), _BF16),
                   jax.ShapeDtypeStruct((n, LANES), _BF16)],
        scratch_shapes=[pltpu.VMEM((tm + 2 * POOL_HALO, POOL_DIM), _F32),
                        pltpu.VMEM((LANES, tm), _F32)],
        compiler_params=pltpu.CompilerParams(dimension_semantics=("parallel",),
                                             vmem_limit_bytes=VMEM_LIMIT),
        name="merge_route",
    )(bias, a, u, u, u, gl, x, edge, *merge_w)


def _moe_kernel(h_ref, route_ref, x_ref, lower_ref, wg_ref, wu_ref, wd_ref, o_ref):
    tm = h_ref.shape[0]
    route = route_ref[...]
    route_f = route.astype(_F32)
    lower = lower_ref[...]
    cum_col = jnp.dot(lower, route, preferred_element_type=_F32)
    rt = route_f.T[0:2 * SUBLANES]
    cum_row = lax.dot_general(rt.astype(_BF16), lower, _NT_DIMS, preferred_element_type=_F32)
    lane = lax.broadcasted_iota(jnp.int32, (tm, LANES), 1)
    key_cols = route_f * (cum_col + 1.0) - 1.0
    key_rows = rt * (cum_row + 1.0) - 1.0

    def experts(g, base, key_row):
        sub = lax.broadcasted_iota(jnp.int32, (MOE_CHUNK, tm), 0)
        gather = jnp.where(key_row == (sub + base).astype(_F32), 1.0, 0.0).astype(_BF16)
        hc = jnp.dot(gather, h_ref[...], preferred_element_type=_F32).astype(_BF16)
        rc = jnp.dot(gather, route, preferred_element_type=_F32)
        yc = None
        for j in range(EXPERTS_PER_GROUP):
            e = g * EXPERTS_PER_GROUP + j
            w = rc[:, ROUTE_HI + j:ROUTE_HI + j + 1] + rc[:, ROUTE_LO + j:ROUTE_LO + j + 1]
            gate = jnp.dot(hc, wg_ref[e], preferred_element_type=_F32)
            up = jnp.dot(hc, wu_ref[e], preferred_element_type=_F32)
            act = gate * _sigmoid(gate) * up * w
            y = jnp.dot(act.astype(_BF16), wd_ref[e], preferred_element_type=_F32)
            yc = y if yc is None else yc + y
        return jnp.concatenate([yc.astype(_BF16),
                                jnp.zeros((MOE_CHUNK_PAD - MOE_CHUNK, D_MODEL), _BF16)], axis=0)

    def scattered(yc, base, key_col):
        col = lax.broadcasted_iota(jnp.int32, (tm, MOE_CHUNK_PAD), 1)
        scatter = jnp.where(key_col == (col + base).astype(_F32), 1.0, 0.0).astype(_BF16)
        return jnp.dot(scatter, yc, preferred_element_type=_F32)

    keys, ycs = [], []
    for g in range(N_EXPERT_GROUPS):
        key_row = key_rows[ROUTE_GROUP + g:ROUTE_GROUP + g + 1]
        key_col = jnp.sum(jnp.where(lane == ROUTE_GROUP + g, key_cols, 0.0),
                          axis=1, keepdims=True)
        keys.append((key_row, key_col))
        ycs.append(experts(g, 0, key_row))
    y = x_ref[...]
    for yc, (_, key_col) in zip(ycs, keys):
        y = y + scattered(yc, 0, key_col)
    o_ref[...] = y

    for g, (key_row, key_col) in enumerate(keys):
        n_g = jnp.sum(rt[ROUTE_GROUP + g:ROUTE_GROUP + g + 1]).astype(jnp.int32)
        n_chunks = (n_g + (MOE_CHUNK - 1)) // MOE_CHUNK

        def body(k, carry, g=g, key_row=key_row, key_col=key_col):
            base = k * MOE_CHUNK
            o_ref[...] += scattered(experts(g, base, key_row), base, key_col)
            return carry

        lax.fori_loop(1, n_chunks, body, 0)


def _moe_call(h2, route, x1, lower, wg, wu, wd):
    n = h2.shape[0]
    tm = lower.shape[0]
    row = lambda w: pl.BlockSpec((tm, w), lambda i: (i, 0))
    return pl.pallas_call(
        _moe_kernel,
        grid=(n // tm,),
        in_specs=[row(D_MODEL), row(LANES), row(D_MODEL), _const_spec(lower.shape),
                  _const_spec(wg.shape), _const_spec(wu.shape), _const_spec(wd.shape)],
        out_specs=row(D_MODEL),
        out_shape=jax.ShapeDtypeStruct((n, D_MODEL), _F32),
        compiler_params=pltpu.CompilerParams(dimension_semantics=("parallel",),
                                             vmem_limit_bytes=VMEM_LIMIT),
        name="moe",
    )(h2, route, x1, lower, wg, wu, wd)


def _pad_heads(w, head_dim):
    lead = w.shape[:-1]
    w = w.reshape(*lead, N_HEADS, head_dim)
    w = jnp.pad(w, [(0, 0)] * len(lead) + [(0, 0), (0, HEAD_PAD - head_dim)])
    return w.reshape(*lead, N_HEADS * HEAD_PAD)


def _rows(v):
    return jnp.broadcast_to(v.astype(_F32)[:, None], (v.shape[0], LANES))


def _prep_layer(l, w_in, w_uq, w_ukv, q_a_g, kv_a_g, q_norm_g, k_norm_g, norm1_g, kshift):
    q_end = Q_LORA_RANK
    kv_end = q_end + KV_LORA_RANK
    kr_end = kv_end + QK_ROPE_DIM
    wi = w_in[l]
    win = wi[:, kr_end:].astype(_BF16)
    wint = wi[:, :kr_end].T.astype(_BF16)
    wuqt = _pad_heads(w_uq[l], QK_HEAD_DIM).T.astype(_BF16)
    wkv = w_ukv[l].reshape(KV_LORA_RANK, N_HEADS, QK_NOPE_DIM + V_HEAD_DIM)
    wukt = wkv[:, :, :QK_NOPE_DIM].reshape(KV_LORA_RANK, -1).T.astype(_BF16)
    wuvt = jnp.pad(wkv[:, :, QK_NOPE_DIM:], ((0, 0), (0, 0), (0, V_ROWS - V_HEAD_DIM)))
    wuvt = wuvt.reshape(KV_LORA_RANK, N_HEADS * V_ROWS).T.astype(_BF16)

    pad = HEAD_PAD - QK_HEAD_DIM
    gq = jnp.pad(q_norm_g[l], (0, pad))
    gk = k_norm_g[l]
    shift_pos = (jnp.arange(HEAD_PAD) == SHIFT_LANE).astype(_F32)
    ones_row = (jnp.arange(V_ROWS) == V_HEAD_DIM).astype(_F32)
    return (norm1_g[l].reshape(1, -1), win, wint, _rows(q_a_g[l]), _rows(kv_a_g[l]),
            wuqt, wukt, wuvt,
            _rows(gq * (ATTN_SCALE * LOG2E)), _rows(gk[:QK_NOPE_DIM]), _rows(gk[QK_NOPE_DIM:]),
            _rows(shift_pos), _rows(shift_pos[QK_HEAD_DIM:] * kshift),
            _rows(jnp.tile(ones_row, N_HEADS)))


def _pool_edge_scales(seq_len):
    t = jnp.concatenate([jnp.arange(POOL_HALO), jnp.arange(seq_len - POOL_HALO, seq_len)])
    per_group = []
    for w in POOL_WINDOWS:
        left = w // 2
        right = w - 1 - left
        cnt = jnp.minimum(t + right + 1, seq_len) - jnp.maximum(t - left, 0)
        per_group.append(jnp.broadcast_to((1.0 / cnt.astype(_F32))[:, None],
                                          (2 * POOL_HALO, POOL_GROUP_DIM)))
    return jnp.concatenate(per_group, axis=1).reshape(2, POOL_HALO, POOL_DIM)


def kernel(x, positions, norm1_g, w_in, q_a_g, kv_a_g, w_uq, w_ukv, q_norm_g, k_norm_g,
           w_pool, pool_scale, w_branch_attn, w_branch_pool, w_out, norm2_g,
           w_router, router_bias, w_expert_gate, w_expert_up, w_expert_down):
    b, s, d = x.shape
    n = b * s
    depth = w_in.shape[0]

    inv_freq = 1.0 / (ROPE_THETA ** (jnp.arange(0, QK_ROPE_DIM, 2, dtype=_F32) / QK_ROPE_DIM))
    tabs = _rope_table_call(positions.astype(_F32).reshape(b, 1, s), _rows(inv_freq))

    wr = jnp.pad(w_router.astype(_F32), ((0, 0), (0, LANES - N_EXPERTS)))
    wrh = wr.astype(_BF16)
    wrl = (wr - wrh.astype(_F32)).astype(_BF16)
    bias = router_bias.astype(_F32)
    edge = _pool_edge_scales(s)
    lower = jnp.tril(jnp.ones((MOE_TILE, MOE_TILE), _F32), -1).astype(_BF16)

    for l in range(depth):
        bound = ((1.02 * QK_HEAD_DIM ** 0.5) * jnp.max(jnp.abs(q_norm_g[l]))
                 * jnp.max(jnp.abs(k_norm_g[l])))
        use_shift = bound <= SCORE_BOUND_LIMIT
        kshift = jnp.where(use_shift, -bound * LOG2E, 0.0)
        consts = _prep_layer(l, w_in, w_uq, w_ukv, q_a_g, kv_a_g, q_norm_g, k_norm_g, norm1_g,
                             kshift)
        qt, k, vt, u, gl = _in_proj_call(x, tabs, consts)
        a = _attention(qt, k, vt, use_shift)
        merge_w = [w_pool[l].astype(_BF16), pool_scale[l].reshape(1, -1),
                   w_branch_attn[l].astype(_BF16), w_branch_pool[l].astype(_BF16),
                   w_out[l].astype(_BF16), norm2_g[l].reshape(1, d), wrh, wrl]
        x1, h2, route = _merge_call(bias, a.reshape(n, -1), u.reshape(n, -1), gl.reshape(n, -1),
                                    x.reshape(n, d), edge, merge_w, s)
        x = _moe_call(h2, route, x1, lower, w_expert_gate[l].astype(_BF16),
                      w_expert_up[l].astype(_BF16),
                      w_expert_down[l].astype(_BF16)).reshape(b, s, d)
    return x
```

```python
import functools

import jax
import jax.numpy as jnp
from jax import lax
from jax.experimental import pallas as pl
from jax.experimental.pallas import tpu as pltpu

D_MODEL = 1024
N_HEADS = 8
QK_NOPE_DIM = 64
QK_ROPE_DIM = 32
QK_HEAD_DIM = QK_NOPE_DIM + QK_ROPE_DIM
V_HEAD_DIM = 64
Q_LORA_RANK = 384
KV_LORA_RANK = 256
ROPE_THETA = 10000.0
ATTN_SCALE = QK_HEAD_DIM ** -0.5
POOL_WINDOWS = (2, 4, 8, 16)
N_POOL_GROUPS = 4
POOL_DIM = D_MODEL // 2
POOL_GROUP_DIM = POOL_DIM // N_POOL_GROUPS
ATTN_OUT_DIM = N_HEADS * V_HEAD_DIM
N_EXPERTS = 16
N_EXPERT_GROUPS = 4
EXPERTS_PER_GROUP = N_EXPERTS // N_EXPERT_GROUPS
D_EXPERT = 256
EPS = 1e-6

LANES = 128
SUBLANES = 8
HEAD_PAD = LANES
QK_PAD_DIM = N_HEADS * HEAD_PAD
HEADS_PER_STEP = 2
POOL_HALO = 8
SHIFT_LANE = QK_HEAD_DIM
BF16_ROWS = 16
V_ROWS = 80
LOG2E = 1.4426950408889634
ROUTE_HI, ROUTE_LO, ROUTE_GROUP = 0, EXPERTS_PER_GROUP, 2 * EXPERTS_PER_GROUP
MOE_TILE = 512
MOE_CHUNK = 160
MOE_CHUNK_PAD = 256
MERGE_SPLIT = 2
SCORE_BOUND_LIMIT = 32.0

_C_U = (0, POOL_DIM)
_C_G = (_C_U[1], _C_U[1] + 2 * D_MODEL)
_R_Q = (0, Q_LORA_RANK)
_R_KV = (_R_Q[1], _R_Q[1] + KV_LORA_RANK)
_R_KR = (_R_KV[1], _R_KV[1] + QK_ROPE_DIM)

VMEM_LIMIT = 56 * 1024 * 1024

_F32 = jnp.float32
_BF16 = jnp.bfloat16
_NT_DIMS = (((1,), (1,)), ((), ()))


def _const_spec(shape):
    nd = len(shape)
    return pl.BlockSpec(shape, lambda *_: (0,) * nd, pipeline_mode=pl.Buffered(1))


def _sigmoid(x):
    return 1.0 / (1.0 + jnp.exp2(x * (-LOG2E)))


def _across(col_ref, n):
    return jnp.tile(col_ref[...], (1, n // LANES))


def _rope_table_kernel(pos_ref, freq_ref, cos_ref, sin_ref):
    ang = _across(freq_ref, pos_ref.shape[2]) * pos_ref[0]
    cos_ref[0] = jnp.cos(ang)
    sin_ref[0] = jnp.sin(ang)


def _rope_table_call(pos, freq):
    b, _, s = pos.shape
    half = QK_ROPE_DIM // 2
    tab = jax.ShapeDtypeStruct((b, half, s), _F32)
    return pl.pallas_call(
        _rope_table_kernel,
        grid=(b,),
        in_specs=[pl.BlockSpec((1, 1, s), lambda i: (i, 0, 0)), _const_spec(freq.shape)],
        out_specs=[pl.BlockSpec((1, half, s), lambda i: (i, 0, 0))] * 2,
        out_shape=[tab, tab],
        compiler_params=pltpu.CompilerParams(dimension_semantics=("parallel",)),
        name="rope_table",
    )(pos, freq)


def _rope_rows(x1, x2, cos, sin):
    return x1 * cos - x2 * sin, x1 * sin + x2 * cos


def _in_proj_kernel(x_ref, cos_ref, sin_ref, g1_ref, win_ref, wint_ref, qagt_ref, kvagt_ref,
                    wuqt_ref, wukt_ref, wuvt_ref, gqt_ref, gknt_ref, gkrt_ref, qpadt_ref,
                    kpadt_ref, vpadt_ref,
                    qt_ref, k_ref, vt_ref, u_ref, gl_ref):
    tm = x_ref.shape[1]
    half = QK_ROPE_DIM // 2
    x = x_ref[0]
    h = x * lax.rsqrt(jnp.mean(x * x, axis=-1, keepdims=True) + EPS) * g1_ref[...]
    h = h.astype(_BF16)
    cos, sin = cos_ref[0], sin_ref[0]

    zt = lax.dot_general(wint_ref[...], h, _NT_DIMS, preferred_element_type=_F32)
    cqt = zt[_R_Q[0]:_R_Q[1]]
    cqt = cqt * lax.rsqrt(jnp.mean(cqt * cqt, axis=0, keepdims=True) + EPS) * _across(qagt_ref, tm)
    qt = jnp.dot(wuqt_ref[...], cqt.astype(_BF16), preferred_element_type=_F32)
    gqt = _across(gqt_ref, tm)
    qpadt = _across(qpadt_ref, tm)
    for hd in range(N_HEADS):
        qh = qt[hd * HEAD_PAD:(hd + 1) * HEAD_PAD]
        r = lax.rsqrt(jnp.sum(qh * qh, axis=0, keepdims=True) / QK_HEAD_DIM + EPS)
        qh = qh * r * gqt
        o1, o2 = _rope_rows(qh[QK_NOPE_DIM:QK_NOPE_DIM + half],
                            qh[QK_NOPE_DIM + half:QK_HEAD_DIM], cos, sin)
        qh = jnp.concatenate([qh[0:QK_NOPE_DIM], o1, o2, qh[QK_HEAD_DIM:]], axis=0) + qpadt
        qt_ref[0, hd * HEAD_PAD:(hd + 1) * HEAD_PAD, :] = qh.astype(_BF16)

    ckvt = zt[_R_KV[0]:_R_KV[1]]
    ckvt = (ckvt * lax.rsqrt(jnp.mean(ckvt * ckvt, axis=0, keepdims=True) + EPS)
            * _across(kvagt_ref, tm))
    ckvt = ckvt.astype(_BF16)
    vt = jnp.dot(wuvt_ref[...], ckvt, preferred_element_type=_F32)
    vt_ref[0] = (vt + _across(vpadt_ref, tm)).astype(_BF16)

    krt = zt[_R_KR[0]:_R_KR[1]]
    kr_ss = jnp.sum(krt * krt, axis=0, keepdims=True)
    krt = krt * _across(gkrt_ref, tm)
    kr1, kr2 = _rope_rows(krt[0:half], krt[half:QK_ROPE_DIM], cos, sin)
    knt = jnp.dot(wukt_ref[...], ckvt, preferred_element_type=_F32)
    gknt = _across(gknt_ref, tm)
    kpadt = _across(kpadt_ref, tm)
    for hd in range(N_HEADS):
        kn = knt[hd * QK_NOPE_DIM:(hd + 1) * QK_NOPE_DIM]
        ss = jnp.sum(kn * kn, axis=0, keepdims=True) + kr_ss
        r = lax.rsqrt(ss / QK_HEAD_DIM + EPS)
        kh = jnp.concatenate([kn * gknt * r, kr1 * r, kr2 * r, kpadt], axis=0)
        k_ref[0, :, hd * HEAD_PAD:(hd + 1) * HEAD_PAD] = kh.T.astype(_BF16)

    gl = jnp.dot(h, win_ref[:, _C_G[0]:_C_G[1]], preferred_element_type=_F32)
    gl_ref[0] = _sigmoid(gl).astype(_BF16)
    u_ref[0] = jnp.dot(h, win_ref[:, _C_U[0]:_C_U[1]], preferred_element_type=_F32)


def _in_proj_call(x, tabs, consts, tm=512):
    b, s, _ = x.shape
    half = QK_ROPE_DIM // 2
    row = lambda w: pl.BlockSpec((1, tm, w), lambda bi, i: (bi, i, 0))
    col = lambda r: pl.BlockSpec((1, r, tm), lambda bi, i: (bi, 0, i))
    return pl.pallas_call(
        _in_proj_kernel,
        grid=(b, s // tm),
        in_specs=[row(D_MODEL), col(half), col(half)] + [_const_spec(c.shape) for c in consts],
        out_specs=[col(QK_PAD_DIM), row(QK_PAD_DIM), col(N_HEADS * V_ROWS), row(POOL_DIM),
                   row(2 * D_MODEL)],
        out_shape=[jax.ShapeDtypeStruct((b, QK_PAD_DIM, s), _BF16),
                   jax.ShapeDtypeStruct((b, s, QK_PAD_DIM), _BF16),
                   jax.ShapeDtypeStruct((b, N_HEADS * V_ROWS, s), _BF16),
                   jax.ShapeDtypeStruct((b, s, POOL_DIM), _F32),
                   jax.ShapeDtypeStruct((b, s, 2 * D_MODEL), _BF16)],
        compiler_params=pltpu.CompilerParams(dimension_semantics=("parallel", "parallel"),
                                             vmem_limit_bytes=VMEM_LIMIT),
        name="in_proj",
    )(x, *tabs, *consts)


def _finish_heads(accs, o_ref):
    ot = jnp.concatenate([a[0:V_HEAD_DIM] / a[V_HEAD_DIM:V_HEAD_DIM + 1] for a in accs], axis=0)
    o_ref[0] = ot.T.astype(o_ref.dtype)


def _attention_shifted_kernel(qt_ref, k_ref, vt_ref, o_ref, *, tk):
    n_kv = k_ref.shape[1] // tk
    chunks = [(hd, c) for hd in range(HEADS_PER_STEP) for c in range(n_kv)]
    qts = [qt_ref[0, hd * HEAD_PAD:(hd + 1) * HEAD_PAD, :] for hd in range(HEADS_PER_STEP)]

    def scores(hd, c):
        return jnp.dot(k_ref[0, c * tk:(c + 1) * tk, hd * HEAD_PAD:(hd + 1) * HEAD_PAD], qts[hd],
                       preferred_element_type=_F32)

    accs = [None] * HEADS_PER_STEP
    st_next = scores(*chunks[0])
    for i, (hd, c) in enumerate(chunks):
        st = st_next
        if i + 1 < len(chunks):
            st_next = scores(*chunks[i + 1])
        o = jnp.dot(vt_ref[0, hd * V_ROWS:(hd + 1) * V_ROWS, c * tk:(c + 1) * tk],
                    jnp.exp2(st).astype(_BF16), preferred_element_type=_F32)
        accs[hd] = o if accs[hd] is None else accs[hd] + o
    _finish_heads(accs, o_ref)


def _attention_online_kernel(qt_ref, k_ref, vt_ref, o_ref, *, tk):
    tq = qt_ref.shape[2]
    n_kv = k_ref.shape[1] // tk
    accs = []
    for hd in range(HEADS_PER_STEP):
        qt = qt_ref[0, hd * HEAD_PAD:(hd + 1) * HEAD_PAD, :]
        m = jnp.full((1, tq), -jnp.inf, _F32)
        acc = jnp.zeros((V_ROWS, tq), _F32)
        for c in range(n_kv):
            keys = slice(c * tk, (c + 1) * tk)
            st = jnp.dot(k_ref[0, keys, hd * HEAD_PAD:(hd + 1) * HEAD_PAD], qt,
                         preferred_element_type=_F32)
            m_new = jnp.maximum(m, jnp.max(st, axis=0, keepdims=True))
            pt = jnp.exp2(st - m_new).astype(_BF16)
            acc = jnp.exp2(m - m_new) * acc + jnp.dot(
                vt_ref[0, hd * V_ROWS:(hd + 1) * V_ROWS, keys], pt, preferred_element_type=_F32)
            m = m_new
        accs.append(acc)
    _finish_heads(accs, o_ref)


def _attention_call(body, qt, k, vt, tq, tk, name):
    b, s, _ = k.shape
    qw = HEADS_PER_STEP * HEAD_PAD
    vw = HEADS_PER_STEP * V_ROWS
    ow = HEADS_PER_STEP * V_HEAD_DIM
    return pl.pallas_call(
        functools.partial(body, tk=tk),
        grid=(b, N_HEADS // HEADS_PER_STEP, s // tq),
        in_specs=[pl.BlockSpec((1, qw, tq), lambda bi, hp, qi: (bi, hp, qi)),
                  pl.BlockSpec((1, s, qw), lambda bi, hp, qi: (bi, 0, hp)),
                  pl.BlockSpec((1, vw, s), lambda bi, hp, qi: (bi, hp, 0))],
        out_specs=pl.BlockSpec((1, tq, ow), lambda bi, hp, qi: (bi, qi, hp)),
        out_shape=jax.ShapeDtypeStruct((b, s, ATTN_OUT_DIM), _BF16),
        compiler_params=pltpu.CompilerParams(
            dimension_semantics=("parallel", "parallel", "arbitrary"),
            vmem_limit_bytes=VMEM_LIMIT),
        name=name,
    )(qt, k, vt)


def _attention(qt, k, vt, use_shift):
    shifted = functools.partial(_attention_call, _attention_shifted_kernel,
                                tq=1024, tk=256, name="attention_shifted")
    online = functools.partial(_attention_call, _attention_online_kernel,
                               tq=512, tk=512, name="attention_online")
    return lax.cond(use_shift, shifted, online, qt, k, vt)


def _route_rows(logit_rows, bias_ref):
    scores = [_sigmoid(r) for r in logit_rows]
    biased = [scores[e] + bias_ref[e] for e in range(N_EXPERTS)]
    gscore = []
    for g in range(N_EXPERT_GROUPS):
        a, b, c, d = biased[g * EXPERTS_PER_GROUP:(g + 1) * EXPERTS_PER_GROUP]
        hi1, lo1 = jnp.maximum(a, b), jnp.minimum(a, b)
        hi2, lo2 = jnp.maximum(c, d), jnp.minimum(c, d)
        top = jnp.maximum(hi1, hi2)
        second = jnp.maximum(jnp.minimum(hi1, hi2), jnp.maximum(lo1, lo2))
        gscore.append(top + second)
    best = gscore[0]
    bg = jnp.zeros_like(best, dtype=jnp.int32)
    for g in range(1, N_EXPERT_GROUPS):
        better = gscore[g] > best
        best = jnp.where(better, gscore[g], best)
        bg = jnp.where(better, g, bg)
    vb, vs = [], []
    for i in range(EXPERTS_PER_GROUP):
        b_i, s_i = biased[i], scores[i]
        for g in range(1, N_EXPERT_GROUPS):
            pick = bg == g
            b_i = jnp.where(pick, biased[g * EXPERTS_PER_GROUP + i], b_i)
            s_i = jnp.where(pick, scores[g * EXPERTS_PER_GROUP + i], s_i)
        vb.append(b_i)
        vs.append(s_i)
    i1 = jnp.zeros_like(bg)
    b1 = vb[0]
    for i in range(1, EXPERTS_PER_GROUP):
        better = vb[i] > b1
        b1 = jnp.where(better, vb[i], b1)
        i1 = jnp.where(better, i, i1)
    i2 = jnp.full_like(bg, -1)
    b2 = jnp.full_like(b1, -jnp.inf)
    for i in range(EXPERTS_PER_GROUP):
        better = (i1 != i) & ((vb[i] > b2) | (i2 < 0))
        b2 = jnp.where(better, vb[i], b2)
        i2 = jnp.where(better, i, i2)
    s1 = vs[0]
    s2 = vs[0]
    for i in range(1, EXPERTS_PER_GROUP):
        s1 = jnp.where(i1 == i, vs[i], s1)
        s2 = jnp.where(i2 == i, vs[i], s2)
    denom = s1 + s2
    in_group = [jnp.where((i1 == i) | (i2 == i), vs[i] / denom, 0.0)
                for i in range(EXPERTS_PER_GROUP)]
    group_onehot = [jnp.where(bg == g, 1.0, 0.0) for g in range(N_EXPERT_GROUPS)]
    return in_group, group_onehot


def _window_sums(e, w, tm):
    n = e.shape[0]
    span = 1
    while span < w:
        e = e + pltpu.roll(e, n - span, 0)
        span *= 2
    first = POOL_HALO - w // 2
    if first:
        e = pltpu.roll(e, n - first, 0)
    return e[0:tm]


def _merge_kernel(bias_ref, a_ref, u_ref, up_ref, un_ref, gl_ref, x_ref, edge_ref,
                  wpool_ref, pscale_ref, wa_ref, wp_ref, wo_ref, g2_ref, wrh_ref, wrl_ref,
                  x1_ref, h2_ref, route_ref, ext_ref, gt_ref, *, n_i):
    tm = u_ref.shape[0]
    sub = tm // MERGE_SPLIT
    i = pl.program_id(0) % n_i
    first_tile = i == 0
    last_tile = i == n_i - 1
    ext_ref[0:POOL_HALO, :] = jnp.where(first_tile, 0.0, up_ref[...])
    ext_ref[POOL_HALO:POOL_HALO + tm, :] = u_ref[...]
    ext_ref[POOL_HALO + tm:, :] = jnp.where(last_tile, 0.0, un_ref[...])
    gt_ref[...] = jnp.zeros_like(gt_ref)

    for part in range(MERGE_SPLIT):
        rows = slice(part * sub, (part + 1) * sub)
        seq_start = first_tile if part == 0 else False
        seq_end = last_tile if part == MERGE_SPLIT - 1 else False

        mixed = []
        for g, w in enumerate(POOL_WINDOWS):
            cols = slice(g * POOL_GROUP_DIM, (g + 1) * POOL_GROUP_DIM)
            sums = _window_sums(ext_ref[part * sub:(part + 1) * sub + 2 * POOL_HALO, cols], w, sub)
            top = sums[0:POOL_HALO] * jnp.where(seq_start, edge_ref[0, :, cols], 1.0 / w)
            bot = sums[sub - POOL_HALO:] * jnp.where(seq_end, edge_ref[1, :, cols], 1.0 / w)
            mean = jnp.concatenate([top, sums[POOL_HALO:sub - POOL_HALO] * (1.0 / w), bot], axis=0)
            pooled = mean - u_ref[rows, cols]
            mixed.append(jnp.dot(pooled.astype(_BF16), wpool_ref[g], preferred_element_type=_F32))
        mixed = jnp.concatenate(mixed, axis=-1) * pscale_ref[...]

        a = jnp.dot(a_ref[rows, :], wa_ref[...], preferred_element_type=_F32)
        p = jnp.dot(mixed.astype(_BF16), wp_ref[...], preferred_element_type=_F32)
        m = (gl_ref[rows, 0:D_MODEL].astype(_F32) * a
             + gl_ref[rows, D_MODEL:2 * D_MODEL].astype(_F32) * p)
        x1 = x_ref[rows, :] + jnp.dot(m.astype(_BF16), wo_ref[...], preferred_element_type=_F32)
        x1_ref[rows, :] = x1

        h2 = x1 * lax.rsqrt(jnp.mean(x1 * x1, axis=-1, keepdims=True) + EPS) * g2_ref[...]
        h2_hi = h2.astype(_BF16)
        h2_ref[rows, :] = h2_hi

        h2_lo = (h2 - h2_hi.astype(_F32)).astype(_BF16)
        logits = (jnp.dot(h2_hi, wrh_ref[...], preferred_element_type=_F32)
                  + jnp.dot(h2_lo, wrh_ref[...], preferred_element_type=_F32)
                  + jnp.dot(h2_hi, wrl_ref[...], preferred_element_type=_F32))
        lt = logits.T
        in_group, group_onehot = _route_rows([lt[e:e + 1, :] for e in range(N_EXPERTS)], bias_ref)
        for j, w in enumerate(in_group):
            hi = w.astype(_BF16).astype(_F32)
            gt_ref[ROUTE_HI + j:ROUTE_HI + j + 1, rows] = hi
            gt_ref[ROUTE_LO + j:ROUTE_LO + j + 1, rows] = w - hi
        for g, onehot in enumerate(group_onehot):
            gt_ref[ROUTE_GROUP + g:ROUTE_GROUP + g + 1, rows] = onehot
        route_ref[rows, :] = gt_ref[:, rows].T.astype(_BF16)


def _merge_call(bias, a, u, gl, x, edge, merge_w, seq_len, tm=512):
    n = x.shape[0]
    hb = tm // POOL_HALO
    n_hb = n // POOL_HALO
    tile = lambda w: pl.BlockSpec((tm, w), lambda s: (s, 0))
    return pl.pallas_call(
        functools.partial(_merge_kernel, n_i=seq_len // tm),
        grid=(n // tm,),
        in_specs=[pl.BlockSpec(memory_space=pltpu.SMEM),
                  tile(ATTN_OUT_DIM), tile(POOL_DIM),
                  pl.BlockSpec((POOL_HALO, POOL_DIM), lambda s: (jnp.maximum(s * hb - 1, 0), 0)),
                  pl.BlockSpec((POOL_HALO, POOL_DIM),
                               lambda s: (jnp.minimum((s + 1) * hb, n_hb - 1), 0)),
                  tile(2 * D_MODEL), tile(D_MODEL), _const_spec(edge.shape)]
                 + [_const_spec(w.shape) for w in merge_w],
        out_specs=[tile(D_MODEL), tile(D_MODEL), tile(LANES)],
        out_shape=[jax.ShapeDtypeStruct((n, D_MODEL), _F32),
                   jax.ShapeDtypeStruct((n, D_MODEL), _BF16),
                   jax.ShapeDtypeStruct((n, LANES), _BF16)],
        scratch_shapes=[pltpu.VMEM((tm + 2 * POOL_HALO, POOL_DIM), _F32),
                        pltpu.VMEM((LANES, tm), _F32)],
        compiler_params=pltpu.CompilerParams(dimension_semantics=("parallel",),
                                             vmem_limit_bytes=VMEM_LIMIT),
        name="merge_route",
    )(bias, a, u, u, u, gl, x, edge, *merge_w)


def _moe_kernel(h_ref, route_ref, x_ref, lower_ref, wg_ref, wu_ref, wd_ref, o_ref):
    tm = h_ref.shape[0]
    route = route_ref[...]
    route_f = route.astype(_F32)
    lower = lower_ref[...]
    cum_col = jnp.dot(lower, route, preferred_element_type=_F32)
    rt = route_f.T[0:2 * SUBLANES]
    cum_row = lax.dot_general(rt.astype(_BF16), lower, _NT_DIMS, preferred_element_type=_F32)
    lane = lax.broadcasted_iota(jnp.int32, (tm, LANES), 1)
    key_cols = route_f * (cum_col + 1.0) - 1.0
    key_rows = rt * (cum_row + 1.0) - 1.0

    def experts(chunks):
        sub = lax.broadcasted_iota(jnp.int32, (MOE_CHUNK, tm), 0)
        hcs, rcs = [], []
        for g, base, key_row in chunks:
            gather = jnp.where(key_row == (sub + base).astype(_F32), 1.0, 0.0).astype(_BF16)
            hcs.append(jnp.dot(gather, h_ref[...], preferred_element_type=_F32).astype(_BF16))
            rcs.append(jnp.dot(gather, route, preferred_element_type=_F32))
        items = [(c, j) for c in range(len(chunks)) for j in range(EXPERTS_PER_GROUP)]

        def gate_up(c, j):
            e = chunks[c][0] * EXPERTS_PER_GROUP + j
            return (jnp.dot(hcs[c], wg_ref[e], preferred_element_type=_F32),
                    jnp.dot(hcs[c], wu_ref[e], preferred_element_type=_F32))

        ycs = [None] * len(chunks)
        ahead = gate_up(*items[0])
        for i, (c, j) in enumerate(items):
            gate, up = ahead
            if i + 1 < len(items):
                ahead = gate_up(*items[i + 1])
            w = (rcs[c][:, ROUTE_HI + j:ROUTE_HI + j + 1]
                 + rcs[c][:, ROUTE_LO + j:ROUTE_LO + j + 1])
            act = gate * _sigmoid(gate) * up * w
            y = jnp.dot(act.astype(_BF16), wd_ref[chunks[c][0] * EXPERTS_PER_GROUP + j],
                        preferred_element_type=_F32)
            ycs[c] = y if ycs[c] is None else ycs[c] + y
        pad = jnp.zeros((MOE_CHUNK_PAD - MOE_CHUNK, D_MODEL), _BF16)
        return [jnp.concatenate([yc.astype(_BF16), pad], axis=0) for yc in ycs]

    def scattered(yc, base, key_col):
        col = lax.broadcasted_iota(jnp.int32, (tm, MOE_CHUNK_PAD), 1)
        scatter = jnp.where(key_col == (col + base).astype(_F32), 1.0, 0.0).astype(_BF16)
        return jnp.dot(scatter, yc, preferred_element_type=_F32)

    keys = []
    for g in range(N_EXPERT_GROUPS):
        key_row = key_rows[ROUTE_GROUP + g:ROUTE_GROUP + g + 1]
        key_col = jnp.sum(jnp.where(lane == ROUTE_GROUP + g, key_cols, 0.0),
                          axis=1, keepdims=True)
        keys.append((key_row, key_col))
    ycs = experts([(g, 0, key_row) for g, (key_row, _) in enumerate(keys)])
    y = x_ref[...]
    for yc, (_, key_col) in zip(ycs, keys):
        y = y + scattered(yc, 0, key_col)
    o_ref[...] = y

    for g, (key_row, key_col) in enumerate(keys):
        n_g = jnp.sum(rt[ROUTE_GROUP + g:ROUTE_GROUP + g + 1]).astype(jnp.int32)
        n_chunks = (n_g + (MOE_CHUNK - 1)) // MOE_CHUNK

        def body(k, carry, g=g, key_row=key_row, key_col=key_col):
            base = k * MOE_CHUNK
            o_ref[...] += scattered(experts([(g, base, key_row)])[0], base, key_col)
            return carry

        lax.fori_loop(1, n_chunks, body, 0)


def _moe_call(h2, route, x1, lower, wg, wu, wd):
    n = h2.shape[0]
    tm = lower.shape[0]
    row = lambda w: pl.BlockSpec((tm, w), lambda i: (i, 0))
    return pl.pallas_call(
        _moe_kernel,
        grid=(n // tm,),
        in_specs=[row(D_MODEL), row(LANES), row(D_MODEL), _const_spec(lower.shape),
                  _const_spec(wg.shape), _const_spec(wu.shape), _const_spec(wd.shape)],
        out_specs=row(D_MODEL),
        out_shape=jax.ShapeDtypeStruct((n, D_MODEL), _F32),
        compiler_params=pltpu.CompilerParams(dimension_semantics=("parallel",),
                                             vmem_limit_bytes=VMEM_LIMIT),
        name="moe",
    )(h2, route, x1, lower, wg, wu, wd)


def _pad_heads(w, head_dim):
    lead = w.shape[:-1]
    w = w.reshape(*lead, N_HEADS, head_dim)
    w = jnp.pad(w, [(0, 0)] * len(lead) + [(0, 0), (0, HEAD_PAD - head_dim)])
    return w.reshape(*lead, N_HEADS * HEAD_PAD)


def _rows(v):
    return jnp.broadcast_to(v.astype(_F32)[:, None], (v.shape[0], LANES))


def _prep_layer(l, w_in, w_uq, w_ukv, q_a_g, kv_a_g, q_norm_g, k_norm_g, norm1_g, kshift):
    q_end = Q_LORA_RANK
    kv_end = q_end + KV_LORA_RANK
    kr_end = kv_end + QK_ROPE_DIM
    wi = w_in[l]
    win = wi[:, kr_end:].astype(_BF16)
    wint = wi[:, :kr_end].T.astype(_BF16)
    wuqt = _pad_heads(w_uq[l], QK_HEAD_DIM).T.astype(_BF16)
    wkv = w_ukv[l].reshape(KV_LORA_RANK, N_HEADS, QK_NOPE_DIM + V_HEAD_DIM)
    wukt = wkv[:, :, :QK_NOPE_DIM].reshape(KV_LORA_RANK, -1).T.astype(_BF16)
    wuvt = jnp.pad(wkv[:, :, QK_NOPE_DIM:], ((0, 0), (0, 0), (0, V_ROWS - V_HEAD_DIM)))
    wuvt = wuvt.reshape(KV_LORA_RANK, N_HEADS * V_ROWS).T.astype(_BF16)

    pad = HEAD_PAD - QK_HEAD_DIM
    gq = jnp.pad(q_norm_g[l], (0, pad))
    gk = k_norm_g[l]
    shift_pos = (jnp.arange(HEAD_PAD) == SHIFT_LANE).astype(_F32)
    ones_row = (jnp.arange(V_ROWS) == V_HEAD_DIM).astype(_F32)
    return (norm1_g[l].reshape(1, -1), win, wint, _rows(q_a_g[l]), _rows(kv_a_g[l]),
            wuqt, wukt, wuvt,
            _rows(gq * (ATTN_SCALE * LOG2E)), _rows(gk[:QK_NOPE_DIM]), _rows(gk[QK_NOPE_DIM:]),
            _rows(shift_pos), _rows(shift_pos[QK_HEAD_DIM:] * kshift),
            _rows(jnp.tile(ones_row, N_HEADS)))


def _pool_edge_scales(seq_len):
    t = jnp.concatenate([jnp.arange(POOL_HALO), jnp.arange(seq_len - POOL_HALO, seq_len)])
    per_group = []
    for w in POOL_WINDOWS:
        left = w // 2
        right = w - 1 - left
        cnt = jnp.minimum(t + right + 1, seq_len) - jnp.maximum(t - left, 0)
        per_group.append(jnp.broadcast_to((1.0 / cnt.astype(_F32))[:, None],
                                          (2 * POOL_HALO, POOL_GROUP_DIM)))
    return jnp.concatenate(per_group, axis=1).reshape(2, POOL_HALO, POOL_DIM)


def kernel(x, positions, norm1_g, w_in, q_a_g, kv_a_g, w_uq, w_ukv, q_norm_g, k_norm_g,
           w_pool, pool_scale, w_branch_attn, w_branch_pool, w_out, norm2_g,
           w_router, router_bias, w_expert_gate, w_expert_up, w_expert_down):
    b, s, d = x.shape
    n = b * s
    depth = w_in.shape[0]

    inv_freq = 1.0 / (ROPE_THETA ** (jnp.arange(0, QK_ROPE_DIM, 2, dtype=_F32) / QK_ROPE_DIM))
    tabs = _rope_table_call(positions.astype(_F32).reshape(b, 1, s), _rows(inv_freq))

    wr = jnp.pad(w_router.astype(_F32), ((0, 0), (0, LANES - N_EXPERTS)))
    wrh = wr.astype(_BF16)
    wrl = (wr - wrh.astype(_F32)).astype(_BF16)
    bias = router_bias.astype(_F32)
    edge = _pool_edge_scales(s)
    lower = jnp.tril(jnp.ones((MOE_TILE, MOE_TILE), _F32), -1).astype(_BF16)

    for l in range(depth):
        bound = ((1.02 * QK_HEAD_DIM ** 0.5) * jnp.max(jnp.abs(q_norm_g[l]))
                 * jnp.max(jnp.abs(k_norm_g[l])))
        use_shift = bound <= SCORE_BOUND_LIMIT
        kshift = jnp.where(use_shift, -bound * LOG2E, 0.0)
        consts = _prep_layer(l, w_in, w_uq, w_ukv, q_a_g, kv_a_g, q_norm_g, k_norm_g, norm1_g,
                             kshift)
        qt, k, vt, u, gl = _in_proj_call(x, tabs, consts)
        a = _attention(qt, k, vt, use_shift)
        merge_w = [w_pool[l].astype(_BF16), pool_scale[l].reshape(1, -1),
                   w_branch_attn[l].astype(_BF16), w_branch_pool[l].astype(_BF16),
                   w_out[l].astype(_BF16), norm2_g[l].reshape(1, d), wrh, wrl]
        x1, h2, route = _merge_call(bias, a.reshape(n, -1), u.reshape(n, -1), gl.reshape(n, -1),
                                    x.reshape(n, d), edge, merge_w, s)
        x = _moe_call(h2, route, x1, lower, w_expert_gate[l].astype(_BF16),
                      w_expert_up[l].astype(_BF16),
                      w_expert_down[l].astype(_BF16)).reshape(b, s, d)
    return x
```

```python
import functools

import jax
import jax.numpy as jnp
from jax import lax
from jax.experimental import pallas as pl
from jax.experimental.pallas import tpu as pltpu

D_MODEL = 1024
N_HEADS = 8
QK_NOPE_DIM = 64
QK_ROPE_DIM = 32
QK_HEAD_DIM = QK_NOPE_DIM + QK_ROPE_DIM
V_HEAD_DIM = 64
Q_LORA_RANK = 384
KV_LORA_RANK = 256
ROPE_THETA = 10000.0
ATTN_SCALE = QK_HEAD_DIM ** -0.5
POOL_WINDOWS = (2, 4, 8, 16)
N_POOL_GROUPS = 4
POOL_DIM = D_MODEL // 2
POOL_GROUP_DIM = POOL_DIM // N_POOL_GROUPS
ATTN_OUT_DIM = N_HEADS * V_HEAD_DIM
N_EXPERTS = 16
N_EXPERT_GROUPS = 4
EXPERTS_PER_GROUP = N_EXPERTS // N_EXPERT_GROUPS
D_EXPERT = 256
EPS = 1e-6

LANES = 128
SUBLANES = 8
HEAD_PAD = LANES
QK_PAD_DIM = N_HEADS * HEAD_PAD
HEADS_PER_STEP = 2
POOL_HALO = 8
SHIFT_LANE = QK_HEAD_DIM
BF16_ROWS = 16
V_ROWS = 80
LOG2E = 1.4426950408889634
ROUTE_HI, ROUTE_LO, ROUTE_GROUP = 0, EXPERTS_PER_GROUP, 2 * EXPERTS_PER_GROUP
MOE_TILE = 512
MOE_CHUNK = 160
MOE_CHUNK_PAD = 256
MERGE_SPLIT = 2
SCORE_BOUND_LIMIT = 32.0

_C_U = (0, POOL_DIM)
_C_G = (_C_U[1], _C_U[1] + 2 * D_MODEL)
_R_Q = (0, Q_LORA_RANK)
_R_KV = (_R_Q[1], _R_Q[1] + KV_LORA_RANK)
_R_KR = (_R_KV[1], _R_KV[1] + QK_ROPE_DIM)

VMEM_LIMIT = 56 * 1024 * 1024

_F32 = jnp.float32
_BF16 = jnp.bfloat16
_NT_DIMS = (((1,), (1,)), ((), ()))


def _const_spec(shape):
    nd = len(shape)
    return pl.BlockSpec(shape, lambda *_: (0,) * nd, pipeline_mode=pl.Buffered(1))


def _sigmoid(x):
    return 1.0 / (1.0 + jnp.exp2(x * (-LOG2E)))


def _across(col_ref, n):
    return jnp.tile(col_ref[...], (1, n // LANES))


def _rope_table_kernel(pos_ref, freq_ref, cos_ref, sin_ref):
    ang = _across(freq_ref, pos_ref.shape[2]) * pos_ref[0]
    cos_ref[0] = jnp.cos(ang)
    sin_ref[0] = jnp.sin(ang)


def _rope_table_call(pos, freq):
    b, _, s = pos.shape
    half = QK_ROPE_DIM // 2
    tab = jax.ShapeDtypeStruct((b, half, s), _F32)
    return pl.pallas_call(
        _rope_table_kernel,
        grid=(b,),
        in_specs=[pl.BlockSpec((1, 1, s), lambda i: (i, 0, 0)), _const_spec(freq.shape)],
        out_specs=[pl.BlockSpec((1, half, s), lambda i: (i, 0, 0))] * 2,
        out_shape=[tab, tab],
        compiler_params=pltpu.CompilerParams(dimension_semantics=("parallel",)),
        name="rope_table",
    )(pos, freq)


def _rope_rows(x1, x2, cos, sin):
    return x1 * cos - x2 * sin, x1 * sin + x2 * cos


def _in_proj_kernel(x_ref, cos_ref, sin_ref, g1_ref, win_ref, wint_ref, qagt_ref, kvagt_ref,
                    wuqt_ref, wukt_ref, wuvt_ref, gqt_ref, gknt_ref, gkrt_ref, qpadt_ref,
                    kpadt_ref, vpadt_ref,
                    qt_ref, k_ref, vt_ref, u_ref, gl_ref):
    tm = x_ref.shape[1]
    half = QK_ROPE_DIM // 2
    x = x_ref[0]
    h = x * lax.rsqrt(jnp.mean(x * x, axis=-1, keepdims=True) + EPS) * g1_ref[...]
    h = h.astype(_BF16)
    cos, sin = cos_ref[0], sin_ref[0]

    zt = lax.dot_general(wint_ref[...], h, _NT_DIMS, preferred_element_type=_F32)
    cqt = zt[_R_Q[0]:_R_Q[1]]
    cqt = cqt * lax.rsqrt(jnp.mean(cqt * cqt, axis=0, keepdims=True) + EPS) * _across(qagt_ref, tm)
    qt = jnp.dot(wuqt_ref[...], cqt.astype(_BF16), preferred_element_type=_F32)
    gqt = _across(gqt_ref, tm)
    qpadt = _across(qpadt_ref, tm)
    for hd in range(N_HEADS):
        qh = qt[hd * HEAD_PAD:(hd + 1) * HEAD_PAD]
        r = lax.rsqrt(jnp.sum(qh * qh, axis=0, keepdims=True) / QK_HEAD_DIM + EPS)
        qh = qh * r * gqt
        o1, o2 = _rope_rows(qh[QK_NOPE_DIM:QK_NOPE_DIM + half],
                            qh[QK_NOPE_DIM + half:QK_HEAD_DIM], cos, sin)
        qh = jnp.concatenate([qh[0:QK_NOPE_DIM], o1, o2, qh[QK_HEAD_DIM:]], axis=0) + qpadt
        qt_ref[0, hd * HEAD_PAD:(hd + 1) * HEAD_PAD, :] = qh.astype(_BF16)

    ckvt = zt[_R_KV[0]:_R_KV[1]]
    ckvt = (ckvt * lax.rsqrt(jnp.mean(ckvt * ckvt, axis=0, keepdims=True) + EPS)
            * _across(kvagt_ref, tm))
    ckvt = ckvt.astype(_BF16)
    vt = jnp.dot(wuvt_ref[...], ckvt, preferred_element_type=_F32)
    vt_ref[0] = (vt + _across(vpadt_ref, tm)).astype(_BF16)

    krt = zt[_R_KR[0]:_R_KR[1]]
    kr_ss = jnp.sum(krt * krt, axis=0, keepdims=True)
    krt = krt * _across(gkrt_ref, tm)
    kr1, kr2 = _rope_rows(krt[0:half], krt[half:QK_ROPE_DIM], cos, sin)
    knt = jnp.dot(wukt_ref[...], ckvt, preferred_element_type=_F32)
    gknt = _across(gknt_ref, tm)
    kpadt = _across(kpadt_ref, tm)
    for hd in range(N_HEADS):
        kn = knt[hd * QK_NOPE_DIM:(hd + 1) * QK_NOPE_DIM]
        ss = jnp.sum(kn * kn, axis=0, keepdims=True) + kr_ss
        r = lax.rsqrt(ss / QK_HEAD_DIM + EPS)
        kh = jnp.concatenate([kn * gknt * r, kr1 * r, kr2 * r, kpadt], axis=0)
        k_ref[0, :, hd * HEAD_PAD:(hd + 1) * HEAD_PAD] = kh.T.astype(_BF16)

    gl_ref[0] = jnp.dot(h, win_ref[:, _C_G[0]:_C_G[1]], preferred_element_type=_F32).astype(_BF16)
    u_ref[0] = jnp.dot(h, win_ref[:, _C_U[0]:_C_U[1]], preferred_element_type=_F32)


def _in_proj_call(x, tabs, consts, tm=1024):
    b, s, _ = x.shape
    half = QK_ROPE_DIM // 2
    row = lambda w: pl.BlockSpec((1, tm, w), lambda bi, i: (bi, i, 0))
    col = lambda r: pl.BlockSpec((1, r, tm), lambda bi, i: (bi, 0, i))
    return pl.pallas_call(
        _in_proj_kernel,
        grid=(b, s // tm),
        in_specs=[row(D_MODEL), col(half), col(half)] + [_const_spec(c.shape) for c in consts],
        out_specs=[col(QK_PAD_DIM), row(QK_PAD_DIM), col(N_HEADS * V_ROWS), row(POOL_DIM),
                   row(2 * D_MODEL)],
        out_shape=[jax.ShapeDtypeStruct((b, QK_PAD_DIM, s), _BF16),
                   jax.ShapeDtypeStruct((b, s, QK_PAD_DIM), _BF16),
                   jax.ShapeDtypeStruct((b, N_HEADS * V_ROWS, s), _BF16),
                   jax.ShapeDtypeStruct((b, s, POOL_DIM), _F32),
                   jax.ShapeDtypeStruct((b, s, 2 * D_MODEL), _BF16)],
        compiler_params=pltpu.CompilerParams(dimension_semantics=("parallel", "parallel"),
                                             vmem_limit_bytes=VMEM_LIMIT),
        name="in_proj",
    )(x, *tabs, *consts)


def _finish_heads(accs, o_ref):
    ot = jnp.concatenate([a[0:V_HEAD_DIM] / a[V_HEAD_DIM:V_HEAD_DIM + 1] for a in accs], axis=0)
    o_ref[0] = ot.T.astype(o_ref.dtype)


def _attention_shifted_kernel(qt_ref, k_ref, vt_ref, o_ref, *, tk):
    n_kv = k_ref.shape[1] // tk
    chunks = [(hd, c) for hd in range(HEADS_PER_STEP) for c in range(n_kv)]
    qts = [qt_ref[0, hd * HEAD_PAD:(hd + 1) * HEAD_PAD, :] for hd in range(HEADS_PER_STEP)]

    def scores(hd, c):
        return jnp.dot(k_ref[0, c * tk:(c + 1) * tk, hd * HEAD_PAD:(hd + 1) * HEAD_PAD], qts[hd],
                       preferred_element_type=_F32)

    accs = [None] * HEADS_PER_STEP
    st_next = scores(*chunks[0])
    for i, (hd, c) in enumerate(chunks):
        st = st_next
        if i + 1 < len(chunks):
            st_next = scores(*chunks[i + 1])
        o = jnp.dot(vt_ref[0, hd * V_ROWS:(hd + 1) * V_ROWS, c * tk:(c + 1) * tk],
                    jnp.exp2(st).astype(_BF16), preferred_element_type=_F32)
        accs[hd] = o if accs[hd] is None else accs[hd] + o
    _finish_heads(accs, o_ref)


def _attention_online_kernel(qt_ref, k_ref, vt_ref, o_ref, *, tk):
    tq = qt_ref.shape[2]
    n_kv = k_ref.shape[1] // tk
    accs = []
    for hd in range(HEADS_PER_STEP):
        qt = qt_ref[0, hd * HEAD_PAD:(hd + 1) * HEAD_PAD, :]
        m = jnp.full((1, tq), -jnp.inf, _F32)
        acc = jnp.zeros((V_ROWS, tq), _F32)
        for c in range(n_kv):
            keys = slice(c * tk, (c + 1) * tk)
            st = jnp.dot(k_ref[0, keys, hd * HEAD_PAD:(hd + 1) * HEAD_PAD], qt,
                         preferred_element_type=_F32)
            m_new = jnp.maximum(m, jnp.max(st, axis=0, keepdims=True))
            pt = jnp.exp2(st - m_new).astype(_BF16)
            acc = jnp.exp2(m - m_new) * acc + jnp.dot(
                vt_ref[0, hd * V_ROWS:(hd + 1) * V_ROWS, keys], pt, preferred_element_type=_F32)
            m = m_new
        accs.append(acc)
    _finish_heads(accs, o_ref)


def _attention_call(body, qt, k, vt, tq, tk, name):
    b, s, _ = k.shape
    qw = HEADS_PER_STEP * HEAD_PAD
    vw = HEADS_PER_STEP * V_ROWS
    ow = HEADS_PER_STEP * V_HEAD_DIM
    return pl.pallas_call(
        functools.partial(body, tk=tk),
        grid=(b, N_HEADS // HEADS_PER_STEP, s // tq),
        in_specs=[pl.BlockSpec((1, qw, tq), lambda bi, hp, qi: (bi, hp, qi)),
                  pl.BlockSpec((1, s, qw), lambda bi, hp, qi: (bi, 0, hp)),
                  pl.BlockSpec((1, vw, s), lambda bi, hp, qi: (bi, hp, 0))],
        out_specs=pl.BlockSpec((1, tq, ow), lambda bi, hp, qi: (bi, qi, hp)),
        out_shape=jax.ShapeDtypeStruct((b, s, ATTN_OUT_DIM), _BF16),
        compiler_params=pltpu.CompilerParams(
            dimension_semantics=("parallel", "parallel", "arbitrary"),
            vmem_limit_bytes=VMEM_LIMIT),
        name=name,
    )(qt, k, vt)


def _attention(qt, k, vt, use_shift):
    shifted = functools.partial(_attention_call, _attention_shifted_kernel,
                                tq=1024, tk=256, name="attention_shifted")
    online = functools.partial(_attention_call, _attention_online_kernel,
                               tq=512, tk=512, name="attention_online")
    return lax.cond(use_shift, shifted, online, qt, k, vt)


def _route_rows(logit_rows, bias_ref):
    scores = [_sigmoid(r) for r in logit_rows]
    biased = [scores[e] + bias_ref[e] for e in range(N_EXPERTS)]
    gscore = []
    for g in range(N_EXPERT_GROUPS):
        a, b, c, d = biased[g * EXPERTS_PER_GROUP:(g + 1) * EXPERTS_PER_GROUP]
        hi1, lo1 = jnp.maximum(a, b), jnp.minimum(a, b)
        hi2, lo2 = jnp.maximum(c, d), jnp.minimum(c, d)
        top = jnp.maximum(hi1, hi2)
        second = jnp.maximum(jnp.minimum(hi1, hi2), jnp.maximum(lo1, lo2))
        gscore.append(top + second)
    best = gscore[0]
    bg = jnp.zeros_like(best, dtype=jnp.int32)
    for g in range(1, N_EXPERT_GROUPS):
        better = gscore[g] > best
        best = jnp.where(better, gscore[g], best)
        bg = jnp.where(better, g, bg)
    vb, vs = [], []
    for i in range(EXPERTS_PER_GROUP):
        b_i, s_i = biased[i], scores[i]
        for g in range(1, N_EXPERT_GROUPS):
            pick = bg == g
            b_i = jnp.where(pick, biased[g * EXPERTS_PER_GROUP + i], b_i)
            s_i = jnp.where(pick, scores[g * EXPERTS_PER_GROUP + i], s_i)
        vb.append(b_i)
        vs.append(s_i)
    i1 = jnp.zeros_like(bg)
    b1 = vb[0]
    for i in range(1, EXPERTS_PER_GROUP):
        better = vb[i] > b1
        b1 = jnp.where(better, vb[i], b1)
        i1 = jnp.where(better, i, i1)
    i2 = jnp.full_like(bg, -1)
    b2 = jnp.full_like(b1, -jnp.inf)
    for i in range(EXPERTS_PER_GROUP):
        better = (i1 != i) & ((vb[i] > b2) | (i2 < 0))
        b2 = jnp.where(better, vb[i], b2)
        i2 = jnp.where(better, i, i2)
    s1 = vs[0]
    s2 = vs[0]
    for i in range(1, EXPERTS_PER_GROUP):
        s1 = jnp.where(i1 == i, vs[i], s1)
        s2 = jnp.where(i2 == i, vs[i], s2)
    denom = s1 + s2
    in_group = [jnp.where((i1 == i) | (i2 == i), vs[i] / denom, 0.0)
                for i in range(EXPERTS_PER_GROUP)]
    group_onehot = [jnp.where(bg == g, 1.0, 0.0) for g in range(N_EXPERT_GROUPS)]
    return in_group, group_onehot


def _window_sums(e, w, tm):
    n = e.shape[0]
    span = 1
    while span < w:
        e = e + pltpu.roll(e, n - span, 0)
        span *= 2
    first = POOL_HALO - w // 2
    if first:
        e = pltpu.roll(e, n - first, 0)
    return e[0:tm]


def _merge_kernel(bias_ref, a_ref, u_ref, up_ref, un_ref, gl_ref, x_ref, edge_ref,
                  wpool_ref, pscale_ref, wa_ref, wp_ref, wo_ref, g2_ref, wrh_ref, wrl_ref,
                  x1_ref, h2_ref, route_ref, ext_ref, gt_ref, *, n_i):
    tm = u_ref.shape[0]
    sub = tm // MERGE_SPLIT
    i = pl.program_id(0) % n_i
    first_tile = i == 0
    last_tile = i == n_i - 1
    ext_ref[0:POOL_HALO, :] = jnp.where(first_tile, 0.0, up_ref[...])
    ext_ref[POOL_HALO:POOL_HALO + tm, :] = u_ref[...]
    ext_ref[POOL_HALO + tm:, :] = jnp.where(last_tile, 0.0, un_ref[...])
    gt_ref[...] = jnp.zeros_like(gt_ref)

    for part in range(MERGE_SPLIT):
        rows = slice(part * sub, (part + 1) * sub)
        seq_start = first_tile if part == 0 else False
        seq_end = last_tile if part == MERGE_SPLIT - 1 else False

        mixed = []
        for g, w in enumerate(POOL_WINDOWS):
            cols = slice(g * POOL_GROUP_DIM, (g + 1) * POOL_GROUP_DIM)
            sums = _window_sums(ext_ref[part * sub:(part + 1) * sub + 2 * POOL_HALO, cols], w, sub)
            top = sums[0:POOL_HALO] * jnp.where(seq_start, edge_ref[0, :, cols], 1.0 / w)
            bot = sums[sub - POOL_HALO:] * jnp.where(seq_end, edge_ref[1, :, cols], 1.0 / w)
            mean = jnp.concatenate([top, sums[POOL_HALO:sub - POOL_HALO] * (1.0 / w), bot], axis=0)
            pooled = mean - u_ref[rows, cols]
            mixed.append(jnp.dot(pooled.astype(_BF16), wpool_ref[g], preferred_element_type=_F32))
        mixed = jnp.concatenate(mixed, axis=-1) * pscale_ref[...]

        a = jnp.dot(a_ref[rows, :], wa_ref[...], preferred_element_type=_F32)
        p = jnp.dot(mixed.astype(_BF16), wp_ref[...], preferred_element_type=_F32)
        m = (_sigmoid(gl_ref[rows, 0:D_MODEL].astype(_F32)) * a
             + _sigmoid(gl_ref[rows, D_MODEL:2 * D_MODEL].astype(_F32)) * p)
        x1 = x_ref[rows, :] + jnp.dot(m.astype(_BF16), wo_ref[...], preferred_element_type=_F32)
        x1_ref[rows, :] = x1

        h2 = x1 * lax.rsqrt(jnp.mean(x1 * x1, axis=-1, keepdims=True) + EPS) * g2_ref[...]
        h2_hi = h2.astype(_BF16)
        h2_ref[rows, :] = h2_hi

        h2_lo = (h2 - h2_hi.astype(_F32)).astype(_BF16)
        logits = (jnp.dot(h2_hi, wrh_ref[...], preferred_element_type=_F32)
                  + jnp.dot(h2_lo, wrh_ref[...], preferred_element_type=_F32)
                  + jnp.dot(h2_hi, wrl_ref[...], preferred_element_type=_F32))
        lt = logits.T
        in_group, group_onehot = _route_rows([lt[e:e + 1, :] for e in range(N_EXPERTS)], bias_ref)
        for j, w in enumerate(in_group):
            hi = w.astype(_BF16).astype(_F32)
            gt_ref[ROUTE_HI + j:ROUTE_HI + j + 1, rows] = hi
            gt_ref[ROUTE_LO + j:ROUTE_LO + j + 1, rows] = w - hi
        for g, onehot in enumerate(group_onehot):
            gt_ref[ROUTE_GROUP + g:ROUTE_GROUP + g + 1, rows] = onehot
        route_ref[rows, :] = gt_ref[:, rows].T.astype(_BF16)


def _merge_call(bias, a, u, gl, x, edge, merge_w, seq_len, tm=512):
    n = x.shape[0]
    hb = tm // POOL_HALO
    n_hb = n // POOL_HALO
    tile = lambda w: pl.BlockSpec((tm, w), lambda s: (s, 0))
    return pl.pallas_call(
        functools.partial(_merge_kernel, n_i=seq_len // tm),
        grid=(n // tm,),
        in_specs=[pl.BlockSpec(memory_space=pltpu.SMEM),
                  tile(ATTN_OUT_DIM), tile(POOL_DIM),
                  pl.BlockSpec((POOL_HALO, POOL_DIM), lambda s: (jnp.maximum(s * hb - 1, 0), 0)),
                  pl.BlockSpec((POOL_HALO, POOL_DIM),
                               lambda s: (jnp.minimum((s + 1) * hb, n_hb - 1), 0)),
                  tile(2 * D_MODEL), tile(D_MODEL), _const_spec(edge.shape)]
                 + [_const_spec(w.shape) for w in merge_w],
        out_specs=[tile(D_MODEL), tile(D_MODEL), tile(LANES)],
        out_shape=[jax.ShapeDtypeStruct((n, D_MODEL), _F32),
                   jax.ShapeDtypeStruct((n, D_MODEL), _BF16),
                   jax.ShapeDtypeStruct((n, LANES), _BF16)],
        scratch_shapes=[pltpu.VMEM((tm + 2 * POOL_HALO, POOL_DIM), _F32),
                        pltpu.VMEM((LANES, tm), _F32)],
        compiler_params=pltpu.CompilerParams(dimension_semantics=("parallel",),
                                             vmem_limit_bytes=VMEM_LIMIT),
        name="merge_route",
    )(bias, a, u, u, u, gl, x, edge, *merge_w)


def _moe_kernel(h_ref, route_ref, x_ref, lower_ref, wg_ref, wu_ref, wd_ref, o_ref):
    tm = h_ref.shape[0]
    route = route_ref[...]
    route_f = route.astype(_F32)
    lower = lower_ref[...]
    cum_col = jnp.dot(lower, route, preferred_element_type=_F32)
    rt = route_f.T[0:2 * SUBLANES]
    cum_row = lax.dot_general(rt.astype(_BF16), lower, _NT_DIMS, preferred_element_type=_F32)
    lane = lax.broadcasted_iota(jnp.int32, (tm, LANES), 1)
    key_cols = route_f * (cum_col + 1.0) - 1.0
    key_rows = rt * (cum_row + 1.0) - 1.0

    def experts(chunks):
        sub = lax.broadcasted_iota(jnp.int32, (MOE_CHUNK, tm), 0)
        hcs, rcs = [], []
        for g, base, key_row in chunks:
            gather = jnp.where(key_row == (sub + base).astype(_F32), 1.0, 0.0).astype(_BF16)
            hcs.append(jnp.dot(gather, h_ref[...], preferred_element_type=_F32).astype(_BF16))
            rcs.append(jnp.dot(gather, route, preferred_element_type=_F32))
        items = [(c, j) for c in range(len(chunks)) for j in range(EXPERTS_PER_GROUP)]

        def gate_up(c, j):
            e = chunks[c][0] * EXPERTS_PER_GROUP + j
            return (jnp.dot(hcs[c], wg_ref[e], preferred_element_type=_F32),
                    jnp.dot(hcs[c], wu_ref[e], preferred_element_type=_F32))

        ycs = [None] * len(chunks)
        ahead = gate_up(*items[0])
        for i, (c, j) in enumerate(items):
            gate, up = ahead
            if i + 1 < len(items):
                ahead = gate_up(*items[i + 1])
            w = (rcs[c][:, ROUTE_HI + j:ROUTE_HI + j + 1]
                 + rcs[c][:, ROUTE_LO + j:ROUTE_LO + j + 1])
            act = gate * _sigmoid(gate) * up * w
            y = jnp.dot(act.astype(_BF16), wd_ref[chunks[c][0] * EXPERTS_PER_GROUP + j],
                        preferred_element_type=_F32)
            ycs[c] = y if ycs[c] is None else ycs[c] + y
        pad = jnp.zeros((MOE_CHUNK_PAD - MOE_CHUNK, D_MODEL), _BF16)
        return [jnp.concatenate([yc.astype(_BF16), pad], axis=0) for yc in ycs]

    def scattered(yc, base, key_col):
        col = lax.broadcasted_iota(jnp.int32, (tm, MOE_CHUNK_PAD), 1)
        scatter = jnp.where(key_col == (col + base).astype(_F32), 1.0, 0.0).astype(_BF16)
        return jnp.dot(scatter, yc, preferred_element_type=_F32)

    keys = []
    for g in range(N_EXPERT_GROUPS):
        key_row = key_rows[ROUTE_GROUP + g:ROUTE_GROUP + g + 1]
        key_col = jnp.sum(jnp.where(lane == ROUTE_GROUP + g, key_cols, 0.0),
                          axis=1, keepdims=True)
        keys.append((key_row, key_col))
    ycs = experts([(g, 0, key_row) for g, (key_row, _) in enumerate(keys)])
    y = x_ref[...]
    for yc, (_, key_col) in zip(ycs, keys):
        y = y + scattered(yc, 0, key_col)
    o_ref[...] = y

    for g, (key_row, key_col) in enumerate(keys):
        n_g = jnp.sum(rt[ROUTE_GROUP + g:ROUTE_GROUP + g + 1]).astype(jnp.int32)
        n_chunks = (n_g + (MOE_CHUNK - 1)) // MOE_CHUNK

        def body(k, carry, g=g, key_row=key_row, key_col=key_col):
            base = k * MOE_CHUNK
            o_ref[...] += scattered(experts([(g, base, key_row)])[0], base, key_col)
            return carry

        lax.fori_loop(1, n_chunks, body, 0)


def _moe_call(h2, route, x1, lower, wg, wu, wd):
    n = h2.shape[0]
    tm = lower.shape[0]
    row = lambda w: pl.BlockSpec((tm, w), lambda i: (i, 0))
    return pl.pallas_call(
        _moe_kernel,
        grid=(n // tm,),
        in_specs=[row(D_MODEL), row(LANES), row(D_MODEL), _const_spec(lower.shape),
                  _const_spec(wg.shape), _const_spec(wu.shape), _const_spec(wd.shape)],
        out_specs=row(D_MODEL),
        out_shape=jax.ShapeDtypeStruct((n, D_MODEL), _F32),
        compiler_params=pltpu.CompilerParams(dimension_semantics=("parallel",),
                                             vmem_limit_bytes=VMEM_LIMIT),
        name="moe",
    )(h2, route, x1, lower, wg, wu, wd)


def _pad_heads(w, head_dim):
    lead = w.shape[:-1]
    w = w.reshape(*lead, N_HEADS, head_dim)
    w = jnp.pad(w, [(0, 0)] * len(lead) + [(0, 0), (0, HEAD_PAD - head_dim)])
    return w.reshape(*lead, N_HEADS * HEAD_PAD)


def _rows(v):
    return jnp.broadcast_to(v.astype(_F32)[:, None], (v.shape[0], LANES))


def _prep_layer(l, w_in, w_uq, w_ukv, q_a_g, kv_a_g, q_norm_g, k_norm_g, norm1_g, kshift):
    q_end = Q_LORA_RANK
    kv_end = q_end + KV_LORA_RANK
    kr_end = kv_end + QK_ROPE_DIM
    wi = w_in[l]
    win = wi[:, kr_end:].astype(_BF16)
    wint = wi[:, :kr_end].T.astype(_BF16)
    wuqt = _pad_heads(w_uq[l], QK_HEAD_DIM).T.astype(_BF16)
    wkv = w_ukv[l].reshape(KV_LORA_RANK, N_HEADS, QK_NOPE_DIM + V_HEAD_DIM)
    wukt = wkv[:, :, :QK_NOPE_DIM].reshape(KV_LORA_RANK, -1).T.astype(_BF16)
    wuvt = jnp.pad(wkv[:, :, QK_NOPE_DIM:], ((0, 0), (0, 0), (0, V_ROWS - V_HEAD_DIM)))
    wuvt = wuvt.reshape(KV_LORA_RANK, N_HEADS * V_ROWS).T.astype(_BF16)

    pad = HEAD_PAD - QK_HEAD_DIM
    gq = jnp.pad(q_norm_g[l], (0, pad))
    gk = k_norm_g[l]
    shift_pos = (jnp.arange(HEAD_PAD) == SHIFT_LANE).astype(_F32)
    ones_row = (jnp.arange(V_ROWS) == V_HEAD_DIM).astype(_F32)
    return (norm1_g[l].reshape(1, -1), win, wint, _rows(q_a_g[l]), _rows(kv_a_g[l]),
            wuqt, wukt, wuvt,
            _rows(gq * (ATTN_SCALE * LOG2E)), _rows(gk[:QK_NOPE_DIM]), _rows(gk[QK_NOPE_DIM:]),
            _rows(shift_pos), _rows(shift_pos[QK_HEAD_DIM:] * kshift),
            _rows(jnp.tile(ones_row, N_HEADS)))


def _pool_edge_scales(seq_len):
    t = jnp.concatenate([jnp.arange(POOL_HALO), jnp.arange(seq_len - POOL_HALO, seq_len)])
    per_group = []
    for w in POOL_WINDOWS:
        left = w // 2
        right = w - 1 - left
        cnt = jnp.minimum(t + right + 1, seq_len) - jnp.maximum(t - left, 0)
        per_group.append(jnp.broadcast_to((1.0 / cnt.astype(_F32))[:, None],
                                          (2 * POOL_HALO, POOL_GROUP_DIM)))
    return jnp.concatenate(per_group, axis=1).reshape(2, POOL_HALO, POOL_DIM)


def kernel(x, positions, norm1_g, w_in, q_a_g, kv_a_g, w_uq, w_ukv, q_norm_g, k_norm_g,
           w_pool, pool_scale, w_branch_attn, w_branch_pool, w_out, norm2_g,
           w_router, router_bias, w_expert_gate, w_expert_up, w_expert_down):
    b, s, d = x.shape
    n = b * s
    depth = w_in.shape[0]

    inv_freq = 1.0 / (ROPE_THETA ** (jnp.arange(0, QK_ROPE_DIM, 2, dtype=_F32) / QK_ROPE_DIM))
    tabs = _rope_table_call(positions.astype(_F32).reshape(b, 1, s), _rows(inv_freq))

    wr = jnp.pad(w_router.astype(_F32), ((0, 0), (0, LANES - N_EXPERTS)))
    wrh = wr.astype(_BF16)
    wrl = (wr - wrh.astype(_F32)).astype(_BF16)
    bias = router_bias.astype(_F32)
    edge = _pool_edge_scales(s)
    lower = jnp.tril(jnp.ones((MOE_TILE, MOE_TILE), _F32), -1).astype(_BF16)

    for l in range(depth):
        bound = ((1.02 * QK_HEAD_DIM ** 0.5) * jnp.max(jnp.abs(q_norm_g[l]))
                 * jnp.max(jnp.abs(k_norm_g[l])))
        use_shift = bound <= SCORE_BOUND_LIMIT
        kshift = jnp.where(use_shift, -bound * LOG2E, 0.0)
        consts = _prep_layer(l, w_in, w_uq, w_ukv, q_a_g, kv_a_g, q_norm_g, k_norm_g, norm1_g,
                             kshift)
        qt, k, vt, u, gl = _in_proj_call(x, tabs, consts)
        a = _attention(qt, k, vt, use_shift)
        merge_w = [w_pool[l].astype(_BF16), pool_scale[l].reshape(1, -1),
                   w_branch_attn[l].astype(_BF16), w_branch_pool[l].astype(_BF16),
                   w_out[l].astype(_BF16), norm2_g[l].reshape(1, d), wrh, wrl]
        x1, h2, route = _merge_call(bias, a.reshape(n, -1), u.reshape(n, -1), gl.reshape(n, -1),
                                    x.reshape(n, d), edge, merge_w, s)
        x = _moe_call(h2, route, x1, lower, w_expert_gate[l].astype(_BF16),
                      w_expert_up[l].astype(_BF16),
                      w_expert_down[l].astype(_BF16)).reshape(b, s, d)
    return x
```

```python
import functools

import jax
import jax.numpy as jnp
from jax import lax
from jax.experimental import pallas as pl
from jax.experimental.pallas import tpu as pltpu

D_MODEL = 1024
N_HEADS = 8
QK_NOPE_DIM = 64
QK_ROPE_DIM = 32
QK_HEAD_DIM = QK_NOPE_DIM + QK_ROPE_DIM
V_HEAD_DIM = 64
Q_LORA_RANK = 384
KV_LORA_RANK = 256
ROPE_THETA = 10000.0
ATTN_SCALE = QK_HEAD_DIM ** -0.5
POOL_WINDOWS = (2, 4, 8, 16)
N_POOL_GROUPS = 4
POOL_DIM = D_MODEL // 2
POOL_GROUP_DIM = POOL_DIM // N_POOL_GROUPS
ATTN_OUT_DIM = N_HEADS * V_HEAD_DIM
N_EXPERTS = 16
N_EXPERT_GROUPS = 4
EXPERTS_PER_GROUP = N_EXPERTS // N_EXPERT_GROUPS
D_EXPERT = 256
EPS = 1e-6

LANES = 128
SUBLANES = 8
HEAD_PAD = LANES
QK_PAD_DIM = N_HEADS * HEAD_PAD
HEADS_PER_STEP = 4
POOL_HALO = 8
SHIFT_LANE = QK_HEAD_DIM
BF16_ROWS = 16
V_ROWS = 80
LOG2E = 1.4426950408889634
ROUTE_HI, ROUTE_LO, ROUTE_GROUP = 0, EXPERTS_PER_GROUP, 2 * EXPERTS_PER_GROUP
MOE_TILE = 512
MOE_CHUNK = 160
MOE_CHUNK_PAD = 256
MERGE_SPLIT = 4
SCORE_BOUND_LIMIT = 32.0

_C_U = (0, POOL_DIM)
_C_G = (_C_U[1], _C_U[1] + 2 * D_MODEL)
_R_Q = (0, Q_LORA_RANK)
_R_KV = (_R_Q[1], _R_Q[1] + KV_LORA_RANK)
_R_KR = (_R_KV[1], _R_KV[1] + QK_ROPE_DIM)

VMEM_LIMIT = 56 * 1024 * 1024

_F32 = jnp.float32
_BF16 = jnp.bfloat16
_NT_DIMS = (((1,), (1,)), ((), ()))


def _const_spec(shape):
    nd = len(shape)
    return pl.BlockSpec(shape, lambda *_: (0,) * nd, pipeline_mode=pl.Buffered(1))


def _sigmoid(x):
    return 1.0 / (1.0 + jnp.exp2(x * (-LOG2E)))


def _across(col_ref, n):
    return jnp.tile(col_ref[...], (1, n // LANES))


def _rope_table_kernel(pos_ref, freq_ref, cos_ref, sin_ref):
    ang = _across(freq_ref, pos_ref.shape[2]) * pos_ref[0]
    cos_ref[0] = jnp.cos(ang)
    sin_ref[0] = jnp.sin(ang)


def _rope_table_call(pos, freq):
    b, _, s = pos.shape
    half = QK_ROPE_DIM // 2
    tab = jax.ShapeDtypeStruct((b, half, s), _F32)
    return pl.pallas_call(
        _rope_table_kernel,
        grid=(b,),
        in_specs=[pl.BlockSpec((1, 1, s), lambda i: (i, 0, 0)), _const_spec(freq.shape)],
        out_specs=[pl.BlockSpec((1, half, s), lambda i: (i, 0, 0))] * 2,
        out_shape=[tab, tab],
        compiler_params=pltpu.CompilerParams(dimension_semantics=("parallel",)),
        name="rope_table",
    )(pos, freq)


def _rope_rows(x1, x2, cos, sin):
    return x1 * cos - x2 * sin, x1 * sin + x2 * cos


def _in_proj_kernel(x_ref, cos_ref, sin_ref, g1_ref, win_ref, wint_ref, qagt_ref, kvagt_ref,
                    wuqt_ref, wukt_ref, wuvt_ref, gqt_ref, gknt_ref, gkrt_ref, qpadt_ref,
                    kpadt_ref, vpadt_ref,
                    qt_ref, k_ref, vt_ref, u_ref, gl_ref):
    tm = x_ref.shape[1]
    half = QK_ROPE_DIM // 2
    x = x_ref[0]
    h = x * lax.rsqrt(jnp.mean(x * x, axis=-1, keepdims=True) + EPS) * g1_ref[...]
    h = h.astype(_BF16)
    cos, sin = cos_ref[0], sin_ref[0]

    zt = lax.dot_general(wint_ref[...], h, _NT_DIMS, preferred_element_type=_F32)
    cqt = zt[_R_Q[0]:_R_Q[1]]
    cqt = cqt * lax.rsqrt(jnp.mean(cqt * cqt, axis=0, keepdims=True) + EPS) * _across(qagt_ref, tm)
    qt = jnp.dot(wuqt_ref[...], cqt.astype(_BF16), preferred_element_type=_F32)
    gqt = _across(gqt_ref, tm)
    qpadt = _across(qpadt_ref, tm)
    for hd in range(N_HEADS):
        qh = qt[hd * HEAD_PAD:(hd + 1) * HEAD_PAD]
        r = lax.rsqrt(jnp.sum(qh * qh, axis=0, keepdims=True) / QK_HEAD_DIM + EPS)
        qh = qh * r * gqt
        o1, o2 = _rope_rows(qh[QK_NOPE_DIM:QK_NOPE_DIM + half],
                            qh[QK_NOPE_DIM + half:QK_HEAD_DIM], cos, sin)
        qh = jnp.concatenate([qh[0:QK_NOPE_DIM], o1, o2, qh[QK_HEAD_DIM:]], axis=0) + qpadt
        qt_ref[0, hd * HEAD_PAD:(hd + 1) * HEAD_PAD, :] = qh.astype(_BF16)

    ckvt = zt[_R_KV[0]:_R_KV[1]]
    ckvt = (ckvt * lax.rsqrt(jnp.mean(ckvt * ckvt, axis=0, keepdims=True) + EPS)
            * _across(kvagt_ref, tm))
    ckvt = ckvt.astype(_BF16)
    vt = jnp.dot(wuvt_ref[...], ckvt, preferred_element_type=_F32)
    vt_ref[0] = (vt + _across(vpadt_ref, tm)).astype(_BF16)

    krt = zt[_R_KR[0]:_R_KR[1]]
    kr_ss = jnp.sum(krt * krt, axis=0, keepdims=True)
    krt = krt * _across(gkrt_ref, tm)
    kr1, kr2 = _rope_rows(krt[0:half], krt[half:QK_ROPE_DIM], cos, sin)
    knt = jnp.dot(wukt_ref[...], ckvt, preferred_element_type=_F32)
    gknt = _across(gknt_ref, tm)
    kpadt = _across(kpadt_ref, tm)
    for hd in range(N_HEADS):
        kn = knt[hd * QK_NOPE_DIM:(hd + 1) * QK_NOPE_DIM]
        ss = jnp.sum(kn * kn, axis=0, keepdims=True) + kr_ss
        r = lax.rsqrt(ss / QK_HEAD_DIM + EPS)
        kh = jnp.concatenate([kn * gknt * r, kr1 * r, kr2 * r, kpadt], axis=0)
        k_ref[0, :, hd * HEAD_PAD:(hd + 1) * HEAD_PAD] = kh.T.astype(_BF16)

    gl_ref[0] = jnp.dot(h, win_ref[:, _C_G[0]:_C_G[1]], preferred_element_type=_F32).astype(_BF16)
    u_ref[0] = jnp.dot(h, win_ref[:, _C_U[0]:_C_U[1]], preferred_element_type=_F32)


def _in_proj_call(x, tabs, consts, tm=1024):
    b, s, _ = x.shape
    half = QK_ROPE_DIM // 2
    row = lambda w: pl.BlockSpec((1, tm, w), lambda bi, i: (bi, i, 0))
    col = lambda r: pl.BlockSpec((1, r, tm), lambda bi, i: (bi, 0, i))
    return pl.pallas_call(
        _in_proj_kernel,
        grid=(b, s // tm),
        in_specs=[row(D_MODEL), col(half), col(half)] + [_const_spec(c.shape) for c in consts],
        out_specs=[col(QK_PAD_DIM), row(QK_PAD_DIM), col(N_HEADS * V_ROWS), row(POOL_DIM),
                   row(2 * D_MODEL)],
        out_shape=[jax.ShapeDtypeStruct((b, QK_PAD_DIM, s), _BF16),
                   jax.ShapeDtypeStruct((b, s, QK_PAD_DIM), _BF16),
                   jax.ShapeDtypeStruct((b, N_HEADS * V_ROWS, s), _BF16),
                   jax.ShapeDtypeStruct((b, s, POOL_DIM), _F32),
                   jax.ShapeDtypeStruct((b, s, 2 * D_MODEL), _BF16)],
        compiler_params=pltpu.CompilerParams(dimension_semantics=("parallel", "parallel"),
                                             vmem_limit_bytes=VMEM_LIMIT),
        name="in_proj",
    )(x, *tabs, *consts)


def _finish_heads(accs, o_ref):
    ot = jnp.concatenate([a[0:V_HEAD_DIM] / a[V_HEAD_DIM:V_HEAD_DIM + 1] for a in accs], axis=0)
    o_ref[0] = ot.T.astype(o_ref.dtype)


def _attention_shifted_kernel(qt_ref, k_ref, vt_ref, o_ref, *, tk):
    n_kv = k_ref.shape[1] // tk
    chunks = [(hd, c) for hd in range(HEADS_PER_STEP) for c in range(n_kv)]
    qts = [qt_ref[0, hd * HEAD_PAD:(hd + 1) * HEAD_PAD, :] for hd in range(HEADS_PER_STEP)]

    def scores(hd, c):
        return jnp.dot(k_ref[0, c * tk:(c + 1) * tk, hd * HEAD_PAD:(hd + 1) * HEAD_PAD], qts[hd],
                       preferred_element_type=_F32)

    accs = [None] * HEADS_PER_STEP
    st_next = scores(*chunks[0])
    for i, (hd, c) in enumerate(chunks):
        st = st_next
        if i + 1 < len(chunks):
            st_next = scores(*chunks[i + 1])
        o = jnp.dot(vt_ref[0, hd * V_ROWS:(hd + 1) * V_ROWS, c * tk:(c + 1) * tk],
                    jnp.exp2(st).astype(_BF16), preferred_element_type=_F32)
        accs[hd] = o if accs[hd] is None else accs[hd] + o
    _finish_heads(accs, o_ref)


def _attention_online_kernel(qt_ref, k_ref, vt_ref, o_ref, *, tk):
    tq = qt_ref.shape[2]
    n_kv = k_ref.shape[1] // tk
    accs = []
    for hd in range(HEADS_PER_STEP):
        qt = qt_ref[0, hd * HEAD_PAD:(hd + 1) * HEAD_PAD, :]
        m = jnp.full((1, tq), -jnp.inf, _F32)
        acc = jnp.zeros((V_ROWS, tq), _F32)
        for c in range(n_kv):
            keys = slice(c * tk, (c + 1) * tk)
            st = jnp.dot(k_ref[0, keys, hd * HEAD_PAD:(hd + 1) * HEAD_PAD], qt,
                         preferred_element_type=_F32)
            m_new = jnp.maximum(m, jnp.max(st, axis=0, keepdims=True))
            pt = jnp.exp2(st - m_new).astype(_BF16)
            acc = jnp.exp2(m - m_new) * acc + jnp.dot(
                vt_ref[0, hd * V_ROWS:(hd + 1) * V_ROWS, keys], pt, preferred_element_type=_F32)
            m = m_new
        accs.append(acc)
    _finish_heads(accs, o_ref)


def _attention_call(body, qt, k, vt, tq, tk, name):
    b, s, _ = k.shape
    qw = HEADS_PER_STEP * HEAD_PAD
    vw = HEADS_PER_STEP * V_ROWS
    ow = HEADS_PER_STEP * V_HEAD_DIM
    return pl.pallas_call(
        functools.partial(body, tk=tk),
        grid=(b, N_HEADS // HEADS_PER_STEP, s // tq),
        in_specs=[pl.BlockSpec((1, qw, tq), lambda bi, hp, qi: (bi, hp, qi)),
                  pl.BlockSpec((1, s, qw), lambda bi, hp, qi: (bi, 0, hp)),
                  pl.BlockSpec((1, vw, s), lambda bi, hp, qi: (bi, hp, 0))],
        out_specs=pl.BlockSpec((1, tq, ow), lambda bi, hp, qi: (bi, qi, hp)),
        out_shape=jax.ShapeDtypeStruct((b, s, ATTN_OUT_DIM), _BF16),
        compiler_params=pltpu.CompilerParams(
            dimension_semantics=("parallel", "parallel", "arbitrary"),
            vmem_limit_bytes=VMEM_LIMIT),
        name=name,
    )(qt, k, vt)


def _attention(qt, k, vt, use_shift):
    shifted = functools.partial(_attention_call, _attention_shifted_kernel,
                                tq=1024, tk=256, name="attention_shifted")
    online = functools.partial(_attention_call, _attention_online_kernel,
                               tq=512, tk=512, name="attention_online")
    return lax.cond(use_shift, shifted, online, qt, k, vt)


def _route_rows(logit_rows, bias_ref):
    scores = [_sigmoid(r) for r in logit_rows]
    biased = [scores[e] + bias_ref[e] for e in range(N_EXPERTS)]
    gscore = []
    for g in range(N_EXPERT_GROUPS):
        a, b, c, d = biased[g * EXPERTS_PER_GROUP:(g + 1) * EXPERTS_PER_GROUP]
        hi1, lo1 = jnp.maximum(a, b), jnp.minimum(a, b)
        hi2, lo2 = jnp.maximum(c, d), jnp.minimum(c, d)
        top = jnp.maximum(hi1, hi2)
        second = jnp.maximum(jnp.minimum(hi1, hi2), jnp.maximum(lo1, lo2))
        gscore.append(top + second)
    best = gscore[0]
    bg = jnp.zeros_like(best, dtype=jnp.int32)
    for g in range(1, N_EXPERT_GROUPS):
        better = gscore[g] > best
        best = jnp.where(better, gscore[g], best)
        bg = jnp.where(better, g, bg)
    vb, vs = [], []
    for i in range(EXPERTS_PER_GROUP):
        b_i, s_i = biased[i], scores[i]
        for g in range(1, N_EXPERT_GROUPS):
            pick = bg == g
            b_i = jnp.where(pick, biased[g * EXPERTS_PER_GROUP + i], b_i)
            s_i = jnp.where(pick, scores[g * EXPERTS_PER_GROUP + i], s_i)
        vb.append(b_i)
        vs.append(s_i)
    i1 = jnp.zeros_like(bg)
    b1 = vb[0]
    for i in range(1, EXPERTS_PER_GROUP):
        better = vb[i] > b1
        b1 = jnp.where(better, vb[i], b1)
        i1 = jnp.where(better, i, i1)
    i2 = jnp.full_like(bg, -1)
    b2 = jnp.full_like(b1, -jnp.inf)
    for i in range(EXPERTS_PER_GROUP):
        better = (i1 != i) & ((vb[i] > b2) | (i2 < 0))
        b2 = jnp.where(better, vb[i], b2)
        i2 = jnp.where(better, i, i2)
    s1 = vs[0]
    s2 = vs[0]
    for i in range(1, EXPERTS_PER_GROUP):
        s1 = jnp.where(i1 == i, vs[i], s1)
        s2 = jnp.where(i2 == i, vs[i], s2)
    denom = s1 + s2
    in_group = [jnp.where((i1 == i) | (i2 == i), vs[i] / denom, 0.0)
                for i in range(EXPERTS_PER_GROUP)]
    group_onehot = [jnp.where(bg == g, 1.0, 0.0) for g in range(N_EXPERT_GROUPS)]
    return in_group, group_onehot


def _window_sums(e, w, tm):
    n = e.shape[0]
    span = 1
    while span < w:
        e = e + pltpu.roll(e, n - span, 0)
        span *= 2
    first = POOL_HALO - w // 2
    if first:
        e = pltpu.roll(e, n - first, 0)
    return e[0:tm]


def _merge_kernel(bias_ref, a_ref, u_ref, up_ref, un_ref, gl_ref, x_ref, edge_ref,
                  wpool_ref, pscale_ref, wa_ref, wp_ref, wo_ref, g2_ref, wrh_ref, wrl_ref,
                  x1_ref, h2_ref, route_ref, ext_ref, gt_ref, *, n_i):
    tm = u_ref.shape[0]
    sub = tm // MERGE_SPLIT
    i = pl.program_id(0) % n_i
    first_tile = i == 0
    last_tile = i == n_i - 1
    ext_ref[0:POOL_HALO, :] = jnp.where(first_tile, 0.0, up_ref[...])
    ext_ref[POOL_HALO:POOL_HALO + tm, :] = u_ref[...]
    ext_ref[POOL_HALO + tm:, :] = jnp.where(last_tile, 0.0, un_ref[...])
    gt_ref[...] = jnp.zeros_like(gt_ref)

    for part in range(MERGE_SPLIT):
        rows = slice(part * sub, (part + 1) * sub)
        seq_start = first_tile if part == 0 else False
        seq_end = last_tile if part == MERGE_SPLIT - 1 else False

        mixed = []
        for g, w in enumerate(POOL_WINDOWS):
            cols = slice(g * POOL_GROUP_DIM, (g + 1) * POOL_GROUP_DIM)
            sums = _window_sums(ext_ref[part * sub:(part + 1) * sub + 2 * POOL_HALO, cols], w, sub)
            top = sums[0:POOL_HALO] * jnp.where(seq_start, edge_ref[0, :, cols], 1.0 / w)
            bot = sums[sub - POOL_HALO:] * jnp.where(seq_end, edge_ref[1, :, cols], 1.0 / w)
            mean = jnp.concatenate([top, sums[POOL_HALO:sub - POOL_HALO] * (1.0 / w), bot], axis=0)
            pooled = mean - u_ref[rows, cols]
            mixed.append(jnp.dot(pooled.astype(_BF16), wpool_ref[g], preferred_element_type=_F32))
        mixed = jnp.concatenate(mixed, axis=-1) * pscale_ref[...]

        a = jnp.dot(a_ref[rows, :], wa_ref[...], preferred_element_type=_F32)
        p = jnp.dot(mixed.astype(_BF16), wp_ref[...], preferred_element_type=_F32)
        m = (_sigmoid(gl_ref[rows, 0:D_MODEL].astype(_F32)) * a
             + _sigmoid(gl_ref[rows, D_MODEL:2 * D_MODEL].astype(_F32)) * p)
        x1 = x_ref[rows, :] + jnp.dot(m.astype(_BF16), wo_ref[...], preferred_element_type=_F32)
        x1_ref[rows, :] = x1

        h2 = x1 * lax.rsqrt(jnp.mean(x1 * x1, axis=-1, keepdims=True) + EPS) * g2_ref[...]
        h2_hi = h2.astype(_BF16)
        h2_ref[rows, :] = h2_hi

        h2_lo = (h2 - h2_hi.astype(_F32)).astype(_BF16)
        logits = (jnp.dot(h2_hi, wrh_ref[...], preferred_element_type=_F32)
                  + jnp.dot(h2_lo, wrh_ref[...], preferred_element_type=_F32)
                  + jnp.dot(h2_hi, wrl_ref[...], preferred_element_type=_F32))
        lt = logits.T
        in_group, group_onehot = _route_rows([lt[e:e + 1, :] for e in range(N_EXPERTS)], bias_ref)
        for j, w in enumerate(in_group):
            hi = w.astype(_BF16).astype(_F32)
            gt_ref[ROUTE_HI + j:ROUTE_HI + j + 1, rows] = hi
            gt_ref[ROUTE_LO + j:ROUTE_LO + j + 1, rows] = w - hi
        for g, onehot in enumerate(group_onehot):
            gt_ref[ROUTE_GROUP + g:ROUTE_GROUP + g + 1, rows] = onehot
        route_ref[rows, :] = gt_ref[:, rows].T.astype(_BF16)


def _merge_call(bias, a, u, gl, x, edge, merge_w, seq_len, tm=1024):
    n = x.shape[0]
    hb = tm // POOL_HALO
    n_hb = n // POOL_HALO
    tile = lambda w: pl.BlockSpec((tm, w), lambda s: (s, 0))
    return pl.pallas_call(
        functools.partial(_merge_kernel, n_i=seq_len // tm),
        grid=(n // tm,),
        in_specs=[pl.BlockSpec(memory_space=pltpu.SMEM),
                  tile(ATTN_OUT_DIM), tile(POOL_DIM),
                  pl.BlockSpec((POOL_HALO, POOL_DIM), lambda s: (jnp.maximum(s * hb - 1, 0), 0)),
                  pl.BlockSpec((POOL_HALO, POOL_DIM),
                               lambda s: (jnp.minimum((s + 1) * hb, n_hb - 1), 0)),
                  tile(2 * D_MODEL), tile(D_MODEL), _const_spec(edge.shape)]
                 + [_const_spec(w.shape) for w in merge_w],
        out_specs=[tile(D_MODEL), tile(D_MODEL), tile(LANES)],
        out_shape=[jax.ShapeDtypeStruct((n, D_MODEL), _F32),
                   jax.ShapeDtypeStruct((n, D_MODEL), _BF16),
                   jax.ShapeDtypeStruct((n, LANES), _BF16)],
        scratch_shapes=[pltpu.VMEM((tm + 2 * POOL_HALO, POOL_DIM), _F32),
                        pltpu.VMEM((LANES, tm), _F32)],
        compiler_params=pltpu.CompilerParams(dimension_semantics=("parallel",),
                                             vmem_limit_bytes=VMEM_LIMIT),
        name="merge_route",
    )(bias, a, u, u, u, gl, x, edge, *merge_w)


def _moe_kernel(h_ref, route_ref, x_ref, lower_ref, wg_ref, wu_ref, wd_ref, o_ref):
    tm = h_ref.shape[0]
    route = route_ref[...]
    route_f = route.astype(_F32)
    lower = lower_ref[...]
    cum_col = jnp.dot(lower, route, preferred_element_type=_F32)
    rt = route_f.T[0:2 * SUBLANES]
    cum_row = lax.dot_general(rt.astype(_BF16), lower, _NT_DIMS, preferred_element_type=_F32)
    lane = lax.broadcasted_iota(jnp.int32, (tm, LANES), 1)
    key_cols = route_f * (cum_col + 1.0) - 1.0
    key_rows = rt * (cum_row + 1.0) - 1.0

    def experts(chunks):
        sub = lax.broadcasted_iota(jnp.int32, (MOE_CHUNK, tm), 0)
        hcs, rcs = [], []
        for g, base, key_row in chunks:
            gather = jnp.where(key_row == (sub + base).astype(_F32), 1.0, 0.0).astype(_BF16)
            hcs.append(jnp.dot(gather, h_ref[...], preferred_element_type=_F32).astype(_BF16))
            rcs.append(jnp.dot(gather, route, preferred_element_type=_F32))
        items = [(c, j) for c in range(len(chunks)) for j in range(EXPERTS_PER_GROUP)]

        def gate_up(c, j):
            e = chunks[c][0] * EXPERTS_PER_GROUP + j
            return (jnp.dot(hcs[c], wg_ref[e], preferred_element_type=_F32),
                    jnp.dot(hcs[c], wu_ref[e], preferred_element_type=_F32))

        ycs = [None] * len(chunks)
        ahead = gate_up(*items[0])
        for i, (c, j) in enumerate(items):
            gate, up = ahead
            if i + 1 < len(items):
                ahead = gate_up(*items[i + 1])
            w = (rcs[c][:, ROUTE_HI + j:ROUTE_HI + j + 1]
                 + rcs[c][:, ROUTE_LO + j:ROUTE_LO + j + 1])
            act = gate * _sigmoid(gate) * up * w
            y = jnp.dot(act.astype(_BF16), wd_ref[chunks[c][0] * EXPERTS_PER_GROUP + j],
                        preferred_element_type=_F32)
            ycs[c] = y if ycs[c] is None else ycs[c] + y
        pad = jnp.zeros((MOE_CHUNK_PAD - MOE_CHUNK, D_MODEL), _BF16)
        return [jnp.concatenate([yc.astype(_BF16), pad], axis=0) for yc in ycs]

    def scattered(yc, base, key_col):
        col = lax.broadcasted_iota(jnp.int32, (tm, MOE_CHUNK_PAD), 1)
        scatter = jnp.where(key_col == (col + base).astype(_F32), 1.0, 0.0).astype(_BF16)
        return jnp.dot(scatter, yc, preferred_element_type=_F32)

    keys = []
    for g in range(N_EXPERT_GROUPS):
        key_row = key_rows[ROUTE_GROUP + g:ROUTE_GROUP + g + 1]
        key_col = jnp.sum(jnp.where(lane == ROUTE_GROUP + g, key_cols, 0.0),
                          axis=1, keepdims=True)
        keys.append((key_row, key_col))
    ycs = experts([(g, 0, key_row) for g, (key_row, _) in enumerate(keys)])
    y = x_ref[...]
    for yc, (_, key_col) in zip(ycs, keys):
        y = y + scattered(yc, 0, key_col)
    o_ref[...] = y

    for g, (key_row, key_col) in enumerate(keys):
        n_g = jnp.sum(rt[ROUTE_GROUP + g:ROUTE_GROUP + g + 1]).astype(jnp.int32)
        n_chunks = (n_g + (MOE_CHUNK - 1)) // MOE_CHUNK

        def body(k, carry, g=g, key_row=key_row, key_col=key_col):
            base = k * MOE_CHUNK
            o_ref[...] += scattered(experts([(g, base, key_row)])[0], base, key_col)
            return carry

        lax.fori_loop(1, n_chunks, body, 0)


def _moe_call(h2, route, x1, lower, wg, wu, wd):
    n = h2.shape[0]
    tm = lower.shape[0]
    row = lambda w: pl.BlockSpec((tm, w), lambda i: (i, 0))
    return pl.pallas_call(
        _moe_kernel,
        grid=(n // tm,),
        in_specs=[row(D_MODEL), row(LANES), row(D_MODEL), _const_spec(lower.shape),
                  _const_spec(wg.shape), _const_spec(wu.shape), _const_spec(wd.shape)],
        out_specs=row(D_MODEL),
        out_shape=jax.ShapeDtypeStruct((n, D_MODEL), _F32),
        compiler_params=pltpu.CompilerParams(dimension_semantics=("parallel",),
                                             vmem_limit_bytes=VMEM_LIMIT),
        name="moe",
    )(h2, route, x1, lower, wg, wu, wd)


def _pad_heads(w, head_dim):
    lead = w.shape[:-1]
    w = w.reshape(*lead, N_HEADS, head_dim)
    w = jnp.pad(w, [(0, 0)] * len(lead) + [(0, 0), (0, HEAD_PAD - head_dim)])
    return w.reshape(*lead, N_HEADS * HEAD_PAD)


def _rows(v):
    return jnp.broadcast_to(v.astype(_F32)[:, None], (v.shape[0], LANES))


def _prep_layer(l, w_in, w_uq, w_ukv, q_a_g, kv_a_g, q_norm_g, k_norm_g, norm1_g, kshift):
    q_end = Q_LORA_RANK
    kv_end = q_end + KV_LORA_RANK
    kr_end = kv_end + QK_ROPE_DIM
    wi = w_in[l]
    win = wi[:, kr_end:].astype(_BF16)
    wint = wi[:, :kr_end].T.astype(_BF16)
    wuqt = _pad_heads(w_uq[l], QK_HEAD_DIM).T.astype(_BF16)
    wkv = w_ukv[l].reshape(KV_LORA_RANK, N_HEADS, QK_NOPE_DIM + V_HEAD_DIM)
    wukt = wkv[:, :, :QK_NOPE_DIM].reshape(KV_LORA_RANK, -1).T.astype(_BF16)
    wuvt = jnp.pad(wkv[:, :, QK_NOPE_DIM:], ((0, 0), (0, 0), (0, V_ROWS - V_HEAD_DIM)))
    wuvt = wuvt.reshape(KV_LORA_RANK, N_HEADS * V_ROWS).T.astype(_BF16)

    pad = HEAD_PAD - QK_HEAD_DIM
    gq = jnp.pad(q_norm_g[l], (0, pad))
    gk = k_norm_g[l]
    shift_pos = (jnp.arange(HEAD_PAD) == SHIFT_LANE).astype(_F32)
    ones_row = (jnp.arange(V_ROWS) == V_HEAD_DIM).astype(_F32)
    return (norm1_g[l].reshape(1, -1), win, wint, _rows(q_a_g[l]), _rows(kv_a_g[l]),
            wuqt, wukt, wuvt,
            _rows(gq * (ATTN_SCALE * LOG2E)), _rows(gk[:QK_NOPE_DIM]), _rows(gk[QK_NOPE_DIM:]),
            _rows(shift_pos), _rows(shift_pos[QK_HEAD_DIM:] * kshift),
            _rows(jnp.tile(ones_row, N_HEADS)))


def _pool_edge_scales(seq_len):
    t = jnp.concatenate([jnp.arange(POOL_HALO), jnp.arange(seq_len - POOL_HALO, seq_len)])
    per_group = []
    for w in POOL_WINDOWS:
        left = w // 2
        right = w - 1 - left
        cnt = jnp.minimum(t + right + 1, seq_len) - jnp.maximum(t - left, 0)
        per_group.append(jnp.broadcast_to((1.0 / cnt.astype(_F32))[:, None],
                                          (2 * POOL_HALO, POOL_GROUP_DIM)))
    return jnp.concatenate(per_group, axis=1).reshape(2, POOL_HALO, POOL_DIM)


def kernel(x, positions, norm1_g, w_in, q_a_g, kv_a_g, w_uq, w_ukv, q_norm_g, k_norm_g,
           w_pool, pool_scale, w_branch_attn, w_branch_pool, w_out, norm2_g,
           w_router, router_bias, w_expert_gate, w_expert_up, w_expert_down):
    b, s, d = x.shape
    n = b * s
    depth = w_in.shape[0]

    inv_freq = 1.0 / (ROPE_THETA ** (jnp.arange(0, QK_ROPE_DIM, 2, dtype=_F32) / QK_ROPE_DIM))
    tabs = _rope_table_call(positions.astype(_F32).reshape(b, 1, s), _rows(inv_freq))

    wr = jnp.pad(w_router.astype(_F32), ((0, 0), (0, LANES - N_EXPERTS)))
    wrh = wr.astype(_BF16)
    wrl = (wr - wrh.astype(_F32)).astype(_BF16)
    bias = router_bias.astype(_F32)
    edge = _pool_edge_scales(s)
    lower = jnp.tril(jnp.ones((MOE_TILE, MOE_TILE), _F32), -1).astype(_BF16)

    for l in range(depth):
        bound = ((1.02 * QK_HEAD_DIM ** 0.5) * jnp.max(jnp.abs(q_norm_g[l]))
                 * jnp.max(jnp.abs(k_norm_g[l])))
        use_shift = bound <= SCORE_BOUND_LIMIT
        kshift = jnp.where(use_shift, -bound * LOG2E, 0.0)
        consts = _prep_layer(l, w_in, w_uq, w_ukv, q_a_g, kv_a_g, q_norm_g, k_norm_g, norm1_g,
                             kshift)
        qt, k, vt, u, gl = _in_proj_call(x, tabs, consts)
        a = _attention(qt, k, vt, use_shift)
        merge_w = [w_pool[l].astype(_BF16), pool_scale[l].reshape(1, -1),
                   w_branch_attn[l].astype(_BF16), w_branch_pool[l].astype(_BF16),
                   w_out[l].astype(_BF16), norm2_g[l].reshape(1, d), wrh, wrl]
        x1, h2, route = _merge_call(bias, a.reshape(n, -1), u.reshape(n, -1), gl.reshape(n, -1),
                                    x.reshape(n, d), edge, merge_w, s)
        x = _moe_call(h2, route, x1, lower, w_expert_gate[l].astype(_BF16),
                      w_expert_up[l].astype(_BF16),
                      w_expert_down[l].astype(_BF16)).reshape(b, s, d)
    return x
```

```python
import functools

import jax
import jax.numpy as jnp
from jax import lax
from jax.experimental import pallas as pl
from jax.experimental.pallas import tpu as pltpu

D_MODEL = 1024
N_HEADS = 8
QK_NOPE_DIM = 64
QK_ROPE_DIM = 32
QK_HEAD_DIM = QK_NOPE_DIM + QK_ROPE_DIM
V_HEAD_DIM = 64
Q_LORA_RANK = 384
KV_LORA_RANK = 256
ROPE_THETA = 10000.0
ATTN_SCALE = QK_HEAD_DIM ** -0.5
POOL_WINDOWS = (2, 4, 8, 16)
N_POOL_GROUPS = 4
POOL_DIM = D_MODEL // 2
POOL_GROUP_DIM = POOL_DIM // N_POOL_GROUPS
ATTN_OUT_DIM = N_HEADS * V_HEAD_DIM
N_EXPERTS = 16
N_EXPERT_GROUPS = 4
EXPERTS_PER_GROUP = N_EXPERTS // N_EXPERT_GROUPS
D_EXPERT = 256
EPS = 1e-6

LANES = 128
SUBLANES = 8
HEAD_PAD = LANES
QK_PAD_DIM = N_HEADS * HEAD_PAD
HEADS_PER_STEP = 4
POOL_HALO = 8
SHIFT_LANE = QK_HEAD_DIM
BF16_ROWS = 16
V_ROWS = 80
LOG2E = 1.4426950408889634
ROUTE_HI, ROUTE_LO, ROUTE_GROUP = 0, EXPERTS_PER_GROUP, 2 * EXPERTS_PER_GROUP
MOE_TILE = 512
MOE_CHUNK = 160
MOE_CHUNK_PAD = 256
MOE_STEP_TILES = 2
MERGE_SPLIT = 4
SCORE_BOUND_LIMIT = 32.0

_C_U = (0, POOL_DIM)
_C_G = (_C_U[1], _C_U[1] + 2 * D_MODEL)
_R_Q = (0, Q_LORA_RANK)
_R_KV = (_R_Q[1], _R_Q[1] + KV_LORA_RANK)
_R_KR = (_R_KV[1], _R_KV[1] + QK_ROPE_DIM)

VMEM_LIMIT = 56 * 1024 * 1024

_F32 = jnp.float32
_BF16 = jnp.bfloat16
_NT_DIMS = (((1,), (1,)), ((), ()))


def _const_spec(shape):
    nd = len(shape)
    return pl.BlockSpec(shape, lambda *_: (0,) * nd, pipeline_mode=pl.Buffered(1))


def _sigmoid(x):
    return 1.0 / (1.0 + jnp.exp2(x * (-LOG2E)))


def _across(col_ref, n):
    return jnp.tile(col_ref[...], (1, n // LANES))


def _rope_table_kernel(pos_ref, freq_ref, cos_ref, sin_ref):
    ang = _across(freq_ref, pos_ref.shape[2]) * pos_ref[0]
    cos_ref[0] = jnp.cos(ang)
    sin_ref[0] = jnp.sin(ang)


def _rope_table_call(pos, freq):
    b, _, s = pos.shape
    half = QK_ROPE_DIM // 2
    tab = jax.ShapeDtypeStruct((b, half, s), _F32)
    return pl.pallas_call(
        _rope_table_kernel,
        grid=(b,),
        in_specs=[pl.BlockSpec((1, 1, s), lambda i: (i, 0, 0)), _const_spec(freq.shape)],
        out_specs=[pl.BlockSpec((1, half, s), lambda i: (i, 0, 0))] * 2,
        out_shape=[tab, tab],
        compiler_params=pltpu.CompilerParams(dimension_semantics=("parallel",)),
        name="rope_table",
    )(pos, freq)


def _rope_rows(x1, x2, cos, sin):
    return x1 * cos - x2 * sin, x1 * sin + x2 * cos


def _in_proj_kernel(x_ref, cos_ref, sin_ref, g1_ref, win_ref, wint_ref, qagt_ref, kvagt_ref,
                    wuqt_ref, wukt_ref, wuvt_ref, gqt_ref, gknt_ref, gkrt_ref, qpadt_ref,
                    kpadt_ref, vpadt_ref,
                    qt_ref, k_ref, vt_ref, u_ref, gl_ref):
    tm = x_ref.shape[1]
    half = QK_ROPE_DIM // 2
    x = x_ref[0]
    h = x * lax.rsqrt(jnp.mean(x * x, axis=-1, keepdims=True) + EPS) * g1_ref[...]
    h = h.astype(_BF16)
    cos, sin = cos_ref[0], sin_ref[0]

    zt = lax.dot_general(wint_ref[...], h, _NT_DIMS, preferred_element_type=_F32)
    cqt = zt[_R_Q[0]:_R_Q[1]]
    cqt = cqt * lax.rsqrt(jnp.mean(cqt * cqt, axis=0, keepdims=True) + EPS) * _across(qagt_ref, tm)
    qt = jnp.dot(wuqt_ref[...], cqt.astype(_BF16), preferred_element_type=_F32)
    gqt = _across(gqt_ref, tm)
    qpadt = _across(qpadt_ref, tm)
    for hd in range(N_HEADS):
        qh = qt[hd * HEAD_PAD:(hd + 1) * HEAD_PAD]
        r = lax.rsqrt(jnp.sum(qh * qh, axis=0, keepdims=True) / QK_HEAD_DIM + EPS)
        qh = qh * r * gqt
        o1, o2 = _rope_rows(qh[QK_NOPE_DIM:QK_NOPE_DIM + half],
                            qh[QK_NOPE_DIM + half:QK_HEAD_DIM], cos, sin)
        qh = jnp.concatenate([qh[0:QK_NOPE_DIM], o1, o2, qh[QK_HEAD_DIM:]], axis=0) + qpadt
        qt_ref[0, hd * HEAD_PAD:(hd + 1) * HEAD_PAD, :] = qh.astype(_BF16)

    ckvt = zt[_R_KV[0]:_R_KV[1]]
    ckvt = (ckvt * lax.rsqrt(jnp.mean(ckvt * ckvt, axis=0, keepdims=True) + EPS)
            * _across(kvagt_ref, tm))
    ckvt = ckvt.astype(_BF16)
    vt = jnp.dot(wuvt_ref[...], ckvt, preferred_element_type=_F32)
    vt_ref[0] = (vt + _across(vpadt_ref, tm)).astype(_BF16)

    krt = zt[_R_KR[0]:_R_KR[1]]
    kr_ss = jnp.sum(krt * krt, axis=0, keepdims=True)
    krt = krt * _across(gkrt_ref, tm)
    kr1, kr2 = _rope_rows(krt[0:half], krt[half:QK_ROPE_DIM], cos, sin)
    knt = jnp.dot(wukt_ref[...], ckvt, preferred_element_type=_F32)
    gknt = _across(gknt_ref, tm)
    kpadt = _across(kpadt_ref, tm)
    for hd in range(N_HEADS):
        kn = knt[hd * QK_NOPE_DIM:(hd + 1) * QK_NOPE_DIM]
        ss = jnp.sum(kn * kn, axis=0, keepdims=True) + kr_ss
        r = lax.rsqrt(ss / QK_HEAD_DIM + EPS)
        kh = jnp.concatenate([kn * gknt * r, kr1 * r, kr2 * r, kpadt], axis=0)
        k_ref[0, :, hd * HEAD_PAD:(hd + 1) * HEAD_PAD] = kh.T.astype(_BF16)

    gl_ref[0] = jnp.dot(h, win_ref[:, _C_G[0]:_C_G[1]], preferred_element_type=_F32).astype(_BF16)
    u_ref[0] = jnp.dot(h, win_ref[:, _C_U[0]:_C_U[1]], preferred_element_type=_F32)


def _in_proj_call(x, tabs, consts, tm=1024):
    b, s, _ = x.shape
    half = QK_ROPE_DIM // 2
    row = lambda w: pl.BlockSpec((1, tm, w), lambda bi, i: (bi, i, 0))
    col = lambda r: pl.BlockSpec((1, r, tm), lambda bi, i: (bi, 0, i))
    return pl.pallas_call(
        _in_proj_kernel,
        grid=(b, s // tm),
        in_specs=[row(D_MODEL), col(half), col(half)] + [_const_spec(c.shape) for c in consts],
        out_specs=[col(QK_PAD_DIM), row(QK_PAD_DIM), col(N_HEADS * V_ROWS), row(POOL_DIM),
                   row(2 * D_MODEL)],
        out_shape=[jax.ShapeDtypeStruct((b, QK_PAD_DIM, s), _BF16),
                   jax.ShapeDtypeStruct((b, s, QK_PAD_DIM), _BF16),
                   jax.ShapeDtypeStruct((b, N_HEADS * V_ROWS, s), _BF16),
                   jax.ShapeDtypeStruct((b, s, POOL_DIM), _F32),
                   jax.ShapeDtypeStruct((b, s, 2 * D_MODEL), _BF16)],
        compiler_params=pltpu.CompilerParams(dimension_semantics=("parallel", "parallel"),
                                             vmem_limit_bytes=VMEM_LIMIT),
        name="in_proj",
    )(x, *tabs, *consts)


def _finish_heads(accs, o_ref):
    ot = jnp.concatenate([a[0:V_HEAD_DIM] / a[V_HEAD_DIM:V_HEAD_DIM + 1] for a in accs], axis=0)
    o_ref[0] = ot.T.astype(o_ref.dtype)


def _attention_shifted_kernel(qt_ref, k_ref, vt_ref, o_ref, *, tk):
    n_kv = k_ref.shape[1] // tk
    chunks = [(hd, c) for hd in range(HEADS_PER_STEP) for c in range(n_kv)]
    qts = [qt_ref[0, hd * HEAD_PAD:(hd + 1) * HEAD_PAD, :] for hd in range(HEADS_PER_STEP)]

    def scores(hd, c):
        return jnp.dot(k_ref[0, c * tk:(c + 1) * tk, hd * HEAD_PAD:(hd + 1) * HEAD_PAD], qts[hd],
                       preferred_element_type=_F32)

    accs = [None] * HEADS_PER_STEP
    st_next = scores(*chunks[0])
    for i, (hd, c) in enumerate(chunks):
        st = st_next
        if i + 1 < len(chunks):
            st_next = scores(*chunks[i + 1])
        o = jnp.dot(vt_ref[0, hd * V_ROWS:(hd + 1) * V_ROWS, c * tk:(c + 1) * tk],
                    jnp.exp2(st).astype(_BF16), preferred_element_type=_F32)
        accs[hd] = o if accs[hd] is None else accs[hd] + o
    _finish_heads(accs, o_ref)


def _attention_online_kernel(qt_ref, k_ref, vt_ref, o_ref, *, tk):
    tq = qt_ref.shape[2]
    n_kv = k_ref.shape[1] // tk
    accs = []
    for hd in range(HEADS_PER_STEP):
        qt = qt_ref[0, hd * HEAD_PAD:(hd + 1) * HEAD_PAD, :]
        m = jnp.full((1, tq), -jnp.inf, _F32)
        acc = jnp.zeros((V_ROWS, tq), _F32)
        for c in range(n_kv):
            keys = slice(c * tk, (c + 1) * tk)
            st = jnp.dot(k_ref[0, keys, hd * HEAD_PAD:(hd + 1) * HEAD_PAD], qt,
                         preferred_element_type=_F32)
            m_new = jnp.maximum(m, jnp.max(st, axis=0, keepdims=True))
            pt = jnp.exp2(st - m_new).astype(_BF16)
            acc = jnp.exp2(m - m_new) * acc + jnp.dot(
                vt_ref[0, hd * V_ROWS:(hd + 1) * V_ROWS, keys], pt, preferred_element_type=_F32)
            m = m_new
        accs.append(acc)
    _finish_heads(accs, o_ref)


def _attention_kernel(flag_ref, qt_ref, k_ref, vt_ref, o_ref, *, tk):
    @pl.when(flag_ref[0] != 0)
    def _():
        _attention_shifted_kernel(qt_ref, k_ref, vt_ref, o_ref, tk=tk)

    @pl.when(flag_ref[0] == 0)
    def _():
        _attention_online_kernel(qt_ref, k_ref, vt_ref, o_ref, tk=tk)


def _attention(qt, k, vt, use_shift, tq=1024, tk=256):
    b, s, _ = k.shape
    qw = HEADS_PER_STEP * HEAD_PAD
    vw = HEADS_PER_STEP * V_ROWS
    ow = HEADS_PER_STEP * V_HEAD_DIM
    return pl.pallas_call(
        functools.partial(_attention_kernel, tk=tk),
        grid=(b, N_HEADS // HEADS_PER_STEP, s // tq),
        in_specs=[pl.BlockSpec(memory_space=pltpu.SMEM),
                  pl.BlockSpec((1, qw, tq), lambda bi, hp, qi: (bi, hp, qi)),
                  pl.BlockSpec((1, s, qw), lambda bi, hp, qi: (bi, 0, hp)),
                  pl.BlockSpec((1, vw, s), lambda bi, hp, qi: (bi, hp, 0))],
        out_specs=pl.BlockSpec((1, tq, ow), lambda bi, hp, qi: (bi, qi, hp)),
        out_shape=jax.ShapeDtypeStruct((b, s, ATTN_OUT_DIM), _BF16),
        compiler_params=pltpu.CompilerParams(
            dimension_semantics=("parallel", "parallel", "arbitrary"),
            vmem_limit_bytes=VMEM_LIMIT),
        name="attention",
    )(use_shift.astype(jnp.int32).reshape(1), qt, k, vt)


def _route_rows(logit_rows, bias_ref):
    scores = [_sigmoid(r) for r in logit_rows]
    biased = [scores[e] + bias_ref[e] for e in range(N_EXPERTS)]
    gscore = []
    for g in range(N_EXPERT_GROUPS):
        a, b, c, d = biased[g * EXPERTS_PER_GROUP:(g + 1) * EXPERTS_PER_GROUP]
        hi1, lo1 = jnp.maximum(a, b), jnp.minimum(a, b)
        hi2, lo2 = jnp.maximum(c, d), jnp.minimum(c, d)
        top = jnp.maximum(hi1, hi2)
        second = jnp.maximum(jnp.minimum(hi1, hi2), jnp.maximum(lo1, lo2))
        gscore.append(top + second)
    best = gscore[0]
    bg = jnp.zeros_like(best, dtype=jnp.int32)
    for g in range(1, N_EXPERT_GROUPS):
        better = gscore[g] > best
        best = jnp.where(better, gscore[g], best)
        bg = jnp.where(better, g, bg)
    vb, vs = [], []
    for i in range(EXPERTS_PER_GROUP):
        b_i, s_i = biased[i], scores[i]
        for g in range(1, N_EXPERT_GROUPS):
            pick = bg == g
            b_i = jnp.where(pick, biased[g * EXPERTS_PER_GROUP + i], b_i)
            s_i = jnp.where(pick, scores[g * EXPERTS_PER_GROUP + i], s_i)
        vb.append(b_i)
        vs.append(s_i)
    i1 = jnp.zeros_like(bg)
    b1 = vb[0]
    for i in range(1, EXPERTS_PER_GROUP):
        better = vb[i] > b1
        b1 = jnp.where(better, vb[i], b1)
        i1 = jnp.where(better, i, i1)
    i2 = jnp.full_like(bg, -1)
    b2 = jnp.full_like(b1, -jnp.inf)
    for i in range(EXPERTS_PER_GROUP):
        better = (i1 != i) & ((vb[i] > b2) | (i2 < 0))
        b2 = jnp.where(better, vb[i], b2)
        i2 = jnp.where(better, i, i2)
    s1 = vs[0]
    s2 = vs[0]
    for i in range(1, EXPERTS_PER_GROUP):
        s1 = jnp.where(i1 == i, vs[i], s1)
        s2 = jnp.where(i2 == i, vs[i], s2)
    denom = s1 + s2
    in_group = [jnp.where((i1 == i) | (i2 == i), vs[i] / denom, 0.0)
                for i in range(EXPERTS_PER_GROUP)]
    group_onehot = [jnp.where(bg == g, 1.0, 0.0) for g in range(N_EXPERT_GROUPS)]
    return in_group, group_onehot


def _window_sums(e, w, tm):
    n = e.shape[0]
    span = 1
    while span < w:
        e = e + pltpu.roll(e, n - span, 0)
        span *= 2
    first = POOL_HALO - w // 2
    if first:
        e = pltpu.roll(e, n - first, 0)
    return e[0:tm]


def _merge_kernel(bias_ref, a_ref, u_ref, up_ref, un_ref, gl_ref, x_ref, edge_ref,
                  wpool_ref, pscale_ref, wa_ref, wp_ref, wo_ref, g2_ref, wrh_ref, wrl_ref,
                  x1_ref, h2_ref, route_ref, ext_ref, gt_ref, *, n_i):
    tm = u_ref.shape[0]
    sub = tm // MERGE_SPLIT
    i = pl.program_id(0) % n_i
    first_tile = i == 0
    last_tile = i == n_i - 1
    ext_ref[0:POOL_HALO, :] = jnp.where(first_tile, 0.0, up_ref[...])
    ext_ref[POOL_HALO:POOL_HALO + tm, :] = u_ref[...]
    ext_ref[POOL_HALO + tm:, :] = jnp.where(last_tile, 0.0, un_ref[...])
    gt_ref[...] = jnp.zeros_like(gt_ref)

    for part in range(MERGE_SPLIT):
        rows = slice(part * sub, (part + 1) * sub)
        seq_start = first_tile if part == 0 else False
        seq_end = last_tile if part == MERGE_SPLIT - 1 else False

        mixed = []
        for g, w in enumerate(POOL_WINDOWS):
            cols = slice(g * POOL_GROUP_DIM, (g + 1) * POOL_GROUP_DIM)
            sums = _window_sums(ext_ref[part * sub:(part + 1) * sub + 2 * POOL_HALO, cols], w, sub)
            top = sums[0:POOL_HALO] * jnp.where(seq_start, edge_ref[0, :, cols], 1.0 / w)
            bot = sums[sub - POOL_HALO:] * jnp.where(seq_end, edge_ref[1, :, cols], 1.0 / w)
            mean = jnp.concatenate([top, sums[POOL_HALO:sub - POOL_HALO] * (1.0 / w), bot], axis=0)
            pooled = mean - u_ref[rows, cols]
            mixed.append(jnp.dot(pooled.astype(_BF16), wpool_ref[g], preferred_element_type=_F32))
        mixed = jnp.concatenate(mixed, axis=-1) * pscale_ref[...]

        a = jnp.dot(a_ref[rows, :], wa_ref[...], preferred_element_type=_F32)
        p = jnp.dot(mixed.astype(_BF16), wp_ref[...], preferred_element_type=_F32)
        m = (_sigmoid(gl_ref[rows, 0:D_MODEL].astype(_F32)) * a
             + _sigmoid(gl_ref[rows, D_MODEL:2 * D_MODEL].astype(_F32)) * p)
        x1 = x_ref[rows, :] + jnp.dot(m.astype(_BF16), wo_ref[...], preferred_element_type=_F32)
        x1_ref[rows, :] = x1

        h2 = x1 * lax.rsqrt(jnp.mean(x1 * x1, axis=-1, keepdims=True) + EPS) * g2_ref[...]
        h2_hi = h2.astype(_BF16)
        h2_ref[rows, :] = h2_hi

        h2_lo = (h2 - h2_hi.astype(_F32)).astype(_BF16)
        logits = (jnp.dot(h2_hi, wrh_ref[...], preferred_element_type=_F32)
                  + jnp.dot(h2_lo, wrh_ref[...], preferred_element_type=_F32)
                  + jnp.dot(h2_hi, wrl_ref[...], preferred_element_type=_F32))
        lt = logits.T
        in_group, group_onehot = _route_rows([lt[e:e + 1, :] for e in range(N_EXPERTS)], bias_ref)
        for j, w in enumerate(in_group):
            hi = w.astype(_BF16).astype(_F32)
            gt_ref[ROUTE_HI + j:ROUTE_HI + j + 1, rows] = hi
            gt_ref[ROUTE_LO + j:ROUTE_LO + j + 1, rows] = w - hi
        for g, onehot in enumerate(group_onehot):
            gt_ref[ROUTE_GROUP + g:ROUTE_GROUP + g + 1, rows] = onehot
        route_ref[rows, :] = gt_ref[:, rows].T.astype(_BF16)


def _merge_call(bias, a, u, gl, x, edge, merge_w, seq_len, tm=1024):
    n = x.shape[0]
    hb = tm // POOL_HALO
    n_hb = n // POOL_HALO
    tile = lambda w: pl.BlockSpec((tm, w), lambda s: (s, 0))
    return pl.pallas_call(
        functools.partial(_merge_kernel, n_i=seq_len // tm),
        grid=(n // tm,),
        in_specs=[pl.BlockSpec(memory_space=pltpu.SMEM),
                  tile(ATTN_OUT_DIM), tile(POOL_DIM),
                  pl.BlockSpec((POOL_HALO, POOL_DIM), lambda s: (jnp.maximum(s * hb - 1, 0), 0)),
                  pl.BlockSpec((POOL_HALO, POOL_DIM),
                               lambda s: (jnp.minimum((s + 1) * hb, n_hb - 1), 0)),
                  tile(2 * D_MODEL), tile(D_MODEL), _const_spec(edge.shape)]
                 + [_const_spec(w.shape) for w in merge_w],
        out_specs=[tile(D_MODEL), tile(D_MODEL), tile(LANES)],
        out_shape=[jax.ShapeDtypeStruct((n, D_MODEL), _F32),
                   jax.ShapeDtypeStruct((n, D_MODEL), _BF16),
                   jax.ShapeDtypeStruct((n, LANES), _BF16)],
        scratch_shapes=[pltpu.VMEM((tm + 2 * POOL_HALO, POOL_DIM), _F32),
                        pltpu.VMEM((LANES, tm), _F32)],
        compiler_params=pltpu.CompilerParams(dimension_semantics=("parallel",),
                                             vmem_limit_bytes=VMEM_LIMIT),
        name="merge_route",
    )(bias, a, u, u, u, gl, x, edge, *merge_w)


def _moe_kernel(h_ref, route_ref, x_ref, lower_ref, wg_ref, wu_ref, wd_ref, o_ref):
    tm = lower_ref.shape[0]
    n_parts = h_ref.shape[0] // tm
    lower = lower_ref[...]
    lane = lax.broadcasted_iota(jnp.int32, (tm, LANES), 1)
    sub = lax.broadcasted_iota(jnp.int32, (MOE_CHUNK, tm), 0)
    col = lax.broadcasted_iota(jnp.int32, (tm, MOE_CHUNK_PAD), 1)
    pad = jnp.zeros((MOE_CHUNK_PAD - MOE_CHUNK, D_MODEL), _BF16)

    def gather(part, base, key_row):
        rows = slice(part * tm, (part + 1) * tm)
        onehot = jnp.where(key_row == (sub + base).astype(_F32), 1.0, 0.0).astype(_BF16)
        return (jnp.dot(onehot, h_ref[rows, :], preferred_element_type=_F32).astype(_BF16),
                jnp.dot(onehot, route_ref[rows, :], preferred_element_type=_F32))

    def experts(groups, hcs, rcs):
        items = [(c, j) for c in range(len(groups)) for j in range(EXPERTS_PER_GROUP)]

        def gate_up(c, j):
            e = groups[c] * EXPERTS_PER_GROUP + j
            return (jnp.dot(hcs[c], wg_ref[e], preferred_element_type=_F32),
                    jnp.dot(hcs[c], wu_ref[e], preferred_element_type=_F32))

        ycs = [None] * len(groups)
        ahead = gate_up(*items[0])
        for i, (c, j) in enumerate(items):
            gate, up = ahead
            if i + 1 < len(items):
                ahead = gate_up(*items[i + 1])
            w = (rcs[c][:, ROUTE_HI + j:ROUTE_HI + j + 1]
                 + rcs[c][:, ROUTE_LO + j:ROUTE_LO + j + 1])
            act = gate * _sigmoid(gate) * up * w
            y = jnp.dot(act.astype(_BF16), wd_ref[groups[c] * EXPERTS_PER_GROUP + j],
                        preferred_element_type=_F32)
            ycs[c] = y if ycs[c] is None else ycs[c] + y
        return ycs

    def scattered(yc, base, key_col):
        yc = jnp.concatenate([yc.astype(_BF16), pad], axis=0)
        onehot = jnp.where(key_col == (col + base).astype(_F32), 1.0, 0.0).astype(_BF16)
        return jnp.dot(onehot, yc, preferred_element_type=_F32)

    keys, counts = [], []
    for part in range(n_parts):
        route = route_ref[part * tm:(part + 1) * tm, :]
        route_f = route.astype(_F32)
        cum_col = jnp.dot(lower, route, preferred_element_type=_F32)
        rt = route_f.T[0:2 * SUBLANES]
        cum_row = lax.dot_general(rt.astype(_BF16), lower, _NT_DIMS,
                                  preferred_element_type=_F32)
        key_cols = route_f * (cum_col + 1.0) - 1.0
        key_rows = rt * (cum_row + 1.0) - 1.0
        part_keys = []
        for g in range(N_EXPERT_GROUPS):
            key_row = key_rows[ROUTE_GROUP + g:ROUTE_GROUP + g + 1]
            key_col = jnp.sum(jnp.where(lane == ROUTE_GROUP + g, key_cols, 0.0),
                              axis=1, keepdims=True)
            part_keys.append((key_row, key_col))
        keys.append(part_keys)
        counts.append([jnp.sum(rt[ROUTE_GROUP + g:ROUTE_GROUP + g + 1]).astype(jnp.int32)
                       for g in range(N_EXPERT_GROUPS)])

    hcs, rcs = [], []
    for g in range(N_EXPERT_GROUPS):
        got = [gather(part, 0, keys[part][g][0]) for part in range(n_parts)]
        hcs.append(jnp.concatenate([h for h, _ in got], axis=0))
        rcs.append(jnp.concatenate([r for _, r in got], axis=0))
    ycs = experts(list(range(N_EXPERT_GROUPS)), hcs, rcs)
    for part in range(n_parts):
        y = x_ref[part * tm:(part + 1) * tm, :]
        for g in range(N_EXPERT_GROUPS):
            y = y + scattered(ycs[g][part * MOE_CHUNK:(part + 1) * MOE_CHUNK], 0, keys[part][g][1])
        o_ref[part * tm:(part + 1) * tm, :] = y

    for part in range(n_parts):
        for g, (key_row, key_col) in enumerate(keys[part]):
            n_chunks = (counts[part][g] + (MOE_CHUNK - 1)) // MOE_CHUNK

            def body(k, carry, part=part, g=g, key_row=key_row, key_col=key_col):
                base = k * MOE_CHUNK
                hc, rc = gather(part, base, key_row)
                yc = experts([g], [hc], [rc])[0]
                o_ref[part * tm:(part + 1) * tm, :] += scattered(yc, base, key_col)
                return carry

            lax.fori_loop(1, n_chunks, body, 0)


def _moe_call(h2, route, x1, lower, wg, wu, wd):
    n = h2.shape[0]
    tm = MOE_STEP_TILES * lower.shape[0]
    row = lambda w: pl.BlockSpec((tm, w), lambda i: (i, 0))
    return pl.pallas_call(
        _moe_kernel,
        grid=(n // tm,),
        in_specs=[row(D_MODEL), row(LANES), row(D_MODEL), _const_spec(lower.shape),
                  _const_spec(wg.shape), _const_spec(wu.shape), _const_spec(wd.shape)],
        out_specs=row(D_MODEL),
        out_shape=jax.ShapeDtypeStruct((n, D_MODEL), _F32),
        compiler_params=pltpu.CompilerParams(dimension_semantics=("parallel",),
                                             vmem_limit_bytes=VMEM_LIMIT),
        name="moe",
    )(h2, route, x1, lower, wg, wu, wd)


def _pad_heads(w, head_dim):
    lead = w.shape[:-1]
    w = w.reshape(*lead, N_HEADS, head_dim)
    w = jnp.pad(w, [(0, 0)] * len(lead) + [(0, 0), (0, HEAD_PAD - head_dim)])
    return w.reshape(*lead, N_HEADS * HEAD_PAD)


def _rows(v):
    return jnp.broadcast_to(v.astype(_F32)[:, None], (v.shape[0], LANES))


def _prep_layer(l, w_in, w_uq, w_ukv, q_a_g, kv_a_g, q_norm_g, k_norm_g, norm1_g, kshift):
    q_end = Q_LORA_RANK
    kv_end = q_end + KV_LORA_RANK
    kr_end = kv_end + QK_ROPE_DIM
    wi = w_in[l]
    win = wi[:, kr_end:].astype(_BF16)
    wint = wi[:, :kr_end].T.astype(_BF16)
    wuqt = _pad_heads(w_uq[l], QK_HEAD_DIM).T.astype(_BF16)
    wkv = w_ukv[l].reshape(KV_LORA_RANK, N_HEADS, QK_NOPE_DIM + V_HEAD_DIM)
    wukt = wkv[:, :, :QK_NOPE_DIM].reshape(KV_LORA_RANK, -1).T.astype(_BF16)
    wuvt = jnp.pad(wkv[:, :, QK_NOPE_DIM:], ((0, 0), (0, 0), (0, V_ROWS - V_HEAD_DIM)))
    wuvt = wuvt.reshape(KV_LORA_RANK, N_HEADS * V_ROWS).T.astype(_BF16)

    pad = HEAD_PAD - QK_HEAD_DIM
    gq = jnp.pad(q_norm_g[l], (0, pad))
    gk = k_norm_g[l]
    shift_pos = (jnp.arange(HEAD_PAD) == SHIFT_LANE).astype(_F32)
    ones_row = (jnp.arange(V_ROWS) == V_HEAD_DIM).astype(_F32)
    return (norm1_g[l].reshape(1, -1), win, wint, _rows(q_a_g[l]), _rows(kv_a_g[l]),
            wuqt, wukt, wuvt,
            _rows(gq * (ATTN_SCALE * LOG2E)), _rows(gk[:QK_NOPE_DIM]), _rows(gk[QK_NOPE_DIM:]),
            _rows(shift_pos), _rows(shift_pos[QK_HEAD_DIM:] * kshift),
            _rows(jnp.tile(ones_row, N_HEADS)))


def _pool_edge_scales(seq_len):
    t = jnp.concatenate([jnp.arange(POOL_HALO), jnp.arange(seq_len - POOL_HALO, seq_len)])
    per_group = []
    for w in POOL_WINDOWS:
        left = w // 2
        right = w - 1 - left
        cnt = jnp.minimum(t + right + 1, seq_len) - jnp.maximum(t - left, 0)
        per_group.append(jnp.broadcast_to((1.0 / cnt.astype(_F32))[:, None],
                                          (2 * POOL_HALO, POOL_GROUP_DIM)))
    return jnp.concatenate(per_group, axis=1).reshape(2, POOL_HALO, POOL_DIM)


def kernel(x, positions, norm1_g, w_in, q_a_g, kv_a_g, w_uq, w_ukv, q_norm_g, k_norm_g,
           w_pool, pool_scale, w_branch_attn, w_branch_pool, w_out, norm2_g,
           w_router, router_bias, w_expert_gate, w_expert_up, w_expert_down):
    b, s, d = x.shape
    n = b * s
    depth = w_in.shape[0]

    inv_freq = 1.0 / (ROPE_THETA ** (jnp.arange(0, QK_ROPE_DIM, 2, dtype=_F32) / QK_ROPE_DIM))
    tabs = _rope_table_call(positions.astype(_F32).reshape(b, 1, s), _rows(inv_freq))

    wr = jnp.pad(w_router.astype(_F32), ((0, 0), (0, LANES - N_EXPERTS)))
    wrh = wr.astype(_BF16)
    wrl = (wr - wrh.astype(_F32)).astype(_BF16)
    bias = router_bias.astype(_F32)
    edge = _pool_edge_scales(s)
    lower = jnp.tril(jnp.ones((MOE_TILE, MOE_TILE), _F32), -1).astype(_BF16)

    for l in range(depth):
        bound = ((1.02 * QK_HEAD_DIM ** 0.5) * jnp.max(jnp.abs(q_norm_g[l]))
                 * jnp.max(jnp.abs(k_norm_g[l])))
        use_shift = bound <= SCORE_BOUND_LIMIT
        kshift = jnp.where(use_shift, -bound * LOG2E, 0.0)
        consts = _prep_layer(l, w_in, w_uq, w_ukv, q_a_g, kv_a_g, q_norm_g, k_norm_g, norm1_g,
                             kshift)
        qt, k, vt, u, gl = _in_proj_call(x, tabs, consts)
        a = _attention(qt, k, vt, use_shift)
        merge_w = [w_pool[l].astype(_BF16), pool_scale[l].reshape(1, -1),
                   w_branch_attn[l].astype(_BF16), w_branch_pool[l].astype(_BF16),
                   w_out[l].astype(_BF16), norm2_g[l].reshape(1, d), wrh, wrl]
        x1, h2, route = _merge_call(bias, a.reshape(n, -1), u.reshape(n, -1), gl.reshape(n, -1),
                                    x.reshape(n, d), edge, merge_w, s)
        x = _moe_call(h2, route, x1, lower, w_expert_gate[l].astype(_BF16),
                      w_expert_up[l].astype(_BF16),
                      w_expert_down[l].astype(_BF16)).reshape(b, s, d)
    return x
```
